```python
import jax, jax.numpy as jnp
from jax import lax
import numpy as np

D_MODEL = 1024
BATCH = 8
SEQ = 2048
DEPTH = 4
DEC_BATCH = 128
DEC_SEQ = 8
PAST_LEN = 16384
PAGE_SIZE = 128

D_A = D_MODEL // 2
CHUNK_A = 128
G_A = 4
DG_A = D_A // G_A
H_RET = 4
DK_RET = D_MODEL // 2 // H_RET
DV_RET = 2 * DK_RET
CHUNK_RET = 128
ROPE_BASE = 10000.0
D_C = D_MODEL // 2
POOL_WINDOWS = (2, 4, 8, 16)
G_C = len(POOL_WINDOWS)
DG_C = D_C // G_C
POOL_BUF = max(POOL_WINDOWS) - 1
D_FF = 2816
N_EXPERTS = 8
TOP_K = 2
D_FF_EXP = 3584
N_DENSE = (DEPTH + 1) // 2
N_MOE = DEPTH // 2
EPS = 1e-6

QK_W = H_RET * DK_RET
V_W = H_RET * DV_RET
IN_SPLITS = (D_A, D_A, QK_W, QK_W, V_W, V_W, D_C, D_MODEL, D_MODEL, D_MODEL)
IN_COLS = sum(IN_SPLITS)
IN_OFFSETS = tuple(int(o) for o in np.cumsum(IN_SPLITS)[:-1])

kernel_name = 'gated_parallel_chunkmlp_retention_pool_decode_step'


def rms_norm(x, g):
    xf = x.astype(jnp.float32)
    y = xf * lax.rsqrt(jnp.mean(xf * xf, axis=-1, keepdims=True) + EPS)
    return (y * g.astype(jnp.float32)).astype(x.dtype)


def layer_norm(x, g, b):
    xf = x.astype(jnp.float32)
    mu = jnp.mean(xf, axis=-1, keepdims=True)
    xc = xf - mu
    var = jnp.mean(xc * xc, axis=-1, keepdims=True)
    return (xc * lax.rsqrt(var + EPS) * g.astype(jnp.float32) + b.astype(jnp.float32)).astype(x.dtype)


def rotary(x, pos):
    half = x.shape[-1] // 2
    inv = ROPE_BASE ** (-jnp.arange(half, dtype=jnp.float32) / half)
    ang = pos.astype(jnp.float32)[:, None] * inv[None, :]
    cos = jnp.cos(ang)[None, :, None, :]
    sin = jnp.sin(ang)[None, :, None, :]
    x1, x2 = x[..., :half], x[..., half:]
    return jnp.concatenate([x1 * cos - x2 * sin, x1 * sin + x2 * cos], axis=-1)


def retention(q, k, v, state):
    B, L = q.shape[0], q.shape[1]
    c = min(CHUNK_RET, L)
    n = L // c
    log_g = jnp.log1p(-jnp.exp2(-5.0 - jnp.arange(H_RET, dtype=jnp.float32)))
    idx = jnp.arange(c, dtype=jnp.float32)
    diff = idx[:, None] - idx[None, :]
    causal = diff >= 0
    decay_in = jnp.where(causal[None], jnp.exp(log_g[:, None, None] * jnp.where(causal, diff, 0.0)[None]), 0.0)
    decay_q = jnp.exp(log_g[:, None] * (idx[None, :] + 1.0))
    decay_k = jnp.exp(log_g[:, None] * (c - 1.0 - idx[None, :]))
    decay_c = jnp.exp(log_g * c)

    def to_chunks(t):
        return t.reshape(B, n, c, H_RET, t.shape[-1]).transpose(1, 0, 3, 2, 4)

    def step(S, qkv):
        qc, kc, vc = qkv
        scores = jnp.einsum('bhid,bhjd->bhij', qc, kc) * decay_in[None]
        inner = jnp.einsum('bhij,bhjv->bhiv', scores, vc)
        cross = jnp.einsum('bhid,bhdv->bhiv', qc, S) * decay_q[None, :, :, None]
        S_new = decay_c[None, :, None, None] * S + jnp.einsum('bhjd,bhjv->bhdv', kc * decay_k[None, :, :, None], vc)
        return S_new, inner + cross

    S, o = lax.scan(step, state, (to_chunks(q), to_chunks(k), to_chunks(v)))
    o = o.transpose(1, 0, 3, 2, 4).reshape(B, L, H_RET, DV_RET)
    return o, S


def chunk_spatial_gating(u, v, w_s, b_s):
    B, L, _ = v.shape
    c = min(CHUNK_A, L)
    n = L // c
    causal = jnp.tril(jnp.ones((c, c), dtype=bool))
    w = jnp.where(causal[None], w_s[:, :c, :c], 0.0).astype(v.dtype)
    vc = v.reshape(B, n, c, G_A, DG_A)
    mixed = jnp.einsum('gts,bnsgd->bntgd', w, vc) + b_s[:, :c].T.astype(v.dtype)[None, None, :, :, None]
    return u * mixed.reshape(B, L, D_A)


def multi_scale_pool(xc, buf, pos0, w_pool, pool_scale):
    B, L, _ = xc.shape
    xx = jnp.concatenate([buf.astype(xc.dtype), xc], axis=1)
    cs = jnp.cumsum(xx.astype(jnp.float32), axis=1)
    cs = jnp.concatenate([jnp.zeros((B, 1, D_C), jnp.float32), cs], axis=1)
    end = cs[:, POOL_BUF + 1:]
    pos = pos0 + jnp.arange(L, dtype=jnp.int32)
    outs = []
    for gi, w in enumerate(POOL_WINDOWS):
        sl = slice(gi * DG_C, (gi + 1) * DG_C)
        start = cs[:, POOL_BUF + 1 - w: POOL_BUF + 1 - w + L, sl]
        cnt = jnp.minimum(w, pos + 1).astype(jnp.float32)[None, :, None]
        outs.append((end[..., sl] - start) / cnt)
    pooled = jnp.concatenate(outs, axis=-1) - xc.astype(jnp.float32)
    mixed = jnp.einsum('blgc,gcd->blgd', pooled.reshape(B, L, G_C, DG_C), w_pool.astype(jnp.float32)).reshape(B, L, D_C)
    new_buf = xx[:, -POOL_BUF:]
    return (mixed * pool_scale.astype(jnp.float32)).astype(xc.dtype), new_buf


def token_mixers(xn, pos0, ret_state, pool_buf, w_in, ln_v_g, ln_v_b, w_spatial, b_spatial,
                 w_pool, pool_scale, w_br_a, w_br_b, w_br_c, w_out):
    B, L, _ = xn.shape
    z = xn @ w_in
    u, v, q, k, vr, gr, xc, ga, gb, gc = jnp.split(z, IN_OFFSETS, axis=-1)
    u = jax.nn.gelu(u)
    v = layer_norm(jax.nn.gelu(v), ln_v_g, ln_v_b)
    a = chunk_spatial_gating(u, v, w_spatial, b_spatial)
    pos = pos0 + jnp.arange(L, dtype=jnp.int32)
    qh = rotary(q.reshape(B, L, H_RET, DK_RET).astype(jnp.float32), pos)
    kh = rotary(k.reshape(B, L, H_RET, DK_RET).astype(jnp.float32), pos) * (DK_RET ** -0.5)
    vh = vr.reshape(B, L, H_RET, DV_RET).astype(jnp.float32)
    o, new_S = retention(qh, kh, vh, ret_state.astype(jnp.float32))
    o = o * lax.rsqrt(jnp.mean(o * o, axis=-1, keepdims=True) + EPS)
    o = (jax.nn.silu(gr.astype(jnp.float32)) * o.reshape(B, L, V_W)).astype(xn.dtype)
    c, new_buf = multi_scale_pool(xc, pool_buf, pos0, w_pool, pool_scale)
    merged = (jax.nn.sigmoid(ga) * (a @ w_br_a)
              + jax.nn.sigmoid(gb) * (o @ w_br_b)
              + jax.nn.sigmoid(gc) * (c @ w_br_c))
    return merged @ w_out, new_S.astype(xn.dtype), new_buf, v


def swiglu(x, wg, wu, wd):
    return (jax.nn.silu(x @ wg) * (x @ wu)) @ wd


def moe_ffn(x, w_router, wg, wu, wd):
    shp = x.shape
    t = x.reshape(-1, D_MODEL)
    logits = (t @ w_router).astype(jnp.float32)
    top_v, top_i = lax.top_k(logits, TOP_K)
    top_w = jax.nn.softmax(top_v, axis=-1)
    combine = jnp.sum(jax.nn.one_hot(top_i, N_EXPERTS, dtype=jnp.float32) * top_w[..., None], axis=1)
    out = jnp.zeros_like(t)
    for e in range(N_EXPERTS):
        out = out + combine[:, e:e + 1].astype(t.dtype) * swiglu(t, wg[e], wu[e], wd[e])
    return out.reshape(shp)


def setup_inputs(seed: int = 0) -> dict:
    key = jax.random.key(seed)
    ks = jax.random.split(key, 32)

    def nrm(k, shape, s):
        return jax.random.normal(k, shape, jnp.float32) * s

    return {
        'x_prompt': nrm(ks[0], (BATCH, SEQ, D_MODEL), 1.0),
        'x_sample': nrm(ks[1], (DEC_BATCH, DEC_SEQ, D_MODEL), 1.0),
        'state_ret': nrm(ks[2], (DEPTH, DEC_BATCH, H_RET, DK_RET, DV_RET), 0.5),
        'state_pool': nrm(ks[3], (DEPTH, DEC_BATCH, POOL_BUF, D_C), 1.0),
        'norm1_g': 1.0 + nrm(ks[4], (DEPTH, D_MODEL), 0.02),
        'w_in': nrm(ks[5], (DEPTH, D_MODEL, IN_COLS), D_MODEL ** -0.5),
        'ln_v_g': 1.0 + nrm(ks[6], (DEPTH, D_A), 0.02),
        'ln_v_b': nrm(ks[7], (DEPTH, D_A), 0.02),
        'w_spatial': nrm(ks[8], (DEPTH, G_A, CHUNK_A, CHUNK_A), CHUNK_A ** -0.5),
        'b_spatial': 1.0 + nrm(ks[9], (DEPTH, G_A, CHUNK_A), 0.02),
        'w_pool': nrm(ks[10], (DEPTH, G_C, DG_C, DG_C), DG_C ** -0.5),
        'pool_scale': 1.0 + nrm(ks[11], (DEPTH, D_C), 0.1),
        'w_br_a': nrm(ks[12], (DEPTH, D_A, D_MODEL), D_A ** -0.5),
        'w_br_b': nrm(ks[13], (DEPTH, V_W, D_MODEL), V_W ** -0.5),
        'w_br_c': nrm(ks[14], (DEPTH, D_C, D_MODEL), D_C ** -0.5),
        'w_out': nrm(ks[15], (DEPTH, D_MODEL, D_MODEL), D_MODEL ** -0.5),
        'norm2_g': 1.0 + nrm(ks[16], (DEPTH, D_MODEL), 0.02),
        'w_ffn_gate': nrm(ks[17], (N_DENSE, D_MODEL, D_FF), D_MODEL ** -0.5),
        'w_ffn_up': nrm(ks[18], (N_DENSE, D_MODEL, D_FF), D_MODEL ** -0.5),
        'w_ffn_down': nrm(ks[19], (N_DENSE, D_FF, D_MODEL), D_FF ** -0.5),
        'w_router': nrm(ks[20], (N_MOE, D_MODEL, N_EXPERTS), D_MODEL ** -0.5),
        'w_exp_gate': nrm(ks[21], (N_MOE, N_EXPERTS, D_MODEL, D_FF_EXP), D_MODEL ** -0.5),
        'w_exp_up': nrm(ks[22], (N_MOE, N_EXPERTS, D_MODEL, D_FF_EXP), D_MODEL ** -0.5),
        'w_exp_down': nrm(ks[23], (N_MOE, N_EXPERTS, D_FF_EXP, D_MODEL), D_FF_EXP ** -0.5),
        'final_norm_g': 1.0 + nrm(ks[24], (D_MODEL,), 0.02),
    }


def reference(x_prompt, x_sample, state_ret, state_pool, norm1_g, w_in, ln_v_g, ln_v_b,
              w_spatial, b_spatial, w_pool, pool_scale, w_br_a, w_br_b, w_br_c, w_out,
              norm2_g, w_ffn_gate, w_ffn_up, w_ffn_down, w_router, w_exp_gate, w_exp_up,
              w_exp_down, final_norm_g):
    hp, hs = x_prompt, x_sample
    ret_p, ret_s, pool_p, pool_s, v_s = [], [], [], [], []
    zero_ret = jnp.zeros((BATCH, H_RET, DK_RET, DV_RET), jnp.float32)
    zero_pool = jnp.zeros((BATCH, POOL_BUF, D_C), x_prompt.dtype)

    for l in range(DEPTH):
        mix_w = (w_in[l], ln_v_g[l], ln_v_b[l], w_spatial[l], b_spatial[l], w_pool[l],
                 pool_scale[l], w_br_a[l], w_br_b[l], w_br_c[l], w_out[l])
        j = l // 2

        def run(h, pos0, S0, buf0):
            m, S, buf, v = token_mixers(rms_norm(h, norm1_g[l]), pos0, S0, buf0, *mix_w)
            h = h + m
            hn = rms_norm(h, norm2_g[l])
            if l % 2 == 0:
                f = swiglu(hn, w_ffn_gate[j], w_ffn_up[j], w_ffn_down[j])
            else:
                f = moe_ffn(hn, w_router[j], w_exp_gate[j], w_exp_up[j], w_exp_down[j])
            return h + f, S, buf, v

        hp, S_p, buf_p, _ = run(hp, 0, zero_ret, zero_pool)
        hs, S_s, buf_s, vrows = run(hs, PAST_LEN, state_ret[l], state_pool[l])
        ret_p.append(S_p)
        ret_s.append(S_s)
        pool_p.append(buf_p)
        pool_s.append(buf_s)
        v_s.append(vrows)

    y_prompt = rms_norm(hp, final_norm_g)
    y_sample = rms_norm(hs, final_norm_g)
    return (y_prompt, y_sample, jnp.stack(ret_p), jnp.stack(ret_s), jnp.stack(pool_p), jnp.stack(pool_s), jnp.stack(v_s))
```

```python
import functools

import jax
import jax.numpy as jnp
import numpy as np
from jax import lax
from jax.experimental import pallas as pl
from jax.experimental.pallas import tpu as pltpu

F32 = jnp.float32
BF16 = jnp.bfloat16

D_MODEL = 1024
DEPTH = 4
PAST_LEN = 16384
D_A = D_MODEL // 2
CHUNK = 128
G_A = 4
DG_A = D_A // G_A
H_RET = 4
DK_RET = D_MODEL // 2 // H_RET
DV_RET = 2 * DK_RET
ROPE_BASE = 10000.0
D_C = D_MODEL // 2
POOL_WINDOWS = (2, 4, 8, 16)
DG_C = D_C // len(POOL_WINDOWS)
POOL_BUF = max(POOL_WINDOWS) - 1
D_FF = 2816
N_EXPERTS = 8
D_FF_EXP = 3584
EPS = 1e-6
QK_W = H_RET * DK_RET
V_W = H_RET * DV_RET
IN_COLS = 2 * D_A + 2 * QK_W + 2 * V_W + D_C + 3 * D_MODEL

LANES = 128
SUBLANES = 8
MIB = 1024 * 1024

_REF_OFF = dict(u=0, v=512, q=1024, k=1536, vr=2048, gr=3072, xc=4096, ga=4608, gb=5632, gc=6656)
_REF_W = dict(u=512, v=512, q=512, k=512, vr=1024, gr=1024, xc=512, ga=1024, gb=1024, gc=1024)
_NEW_ORDER = ("ga", "gb", "gc", "vr", "gr", "u", "v", "q", "k", "xc")
BLK_GA, BLK_GB, BLK_GC, BLK_VR, BLK_GR = 0, 1, 2, 3, 4
BLK_U, BLK_V, BLK_Q, BLK_K, BLK_XC = 10, 11, 12, 13, 14


def _params(sem, vmem_mib):
    return pltpu.CompilerParams(dimension_semantics=sem, vmem_limit_bytes=vmem_mib * MIB)


def _rms(x, g):
    return x * lax.rsqrt(jnp.mean(x * x, axis=-1, keepdims=True) + EPS) * g


def _layer_norm(x, g, b):
    mu = jnp.mean(x, axis=-1, keepdims=True)
    xc = x - mu
    var = jnp.mean(xc * xc, axis=-1, keepdims=True)
    return xc * lax.rsqrt(var + EPS) * g + b


def _dot(a, b):
    return jnp.dot(a, b, preferred_element_type=F32)


def _dot_nt(a, b):
    return lax.dot_general(a, b, (((1,), (1,)), ((), ())), preferred_element_type=F32)


def _dot_tn(a, b):
    return lax.dot_general(a, b, (((0,), (0,)), ((), ())), preferred_element_type=F32)


def _rotary(x, cos2, sin2):
    return x * cos2 + pltpu.roll(x, DK_RET // 2, 1) * sin2


def _inproj_body(x_ref, g_ref, w_ref, o_ref, xn_ref):
    @pl.when(pl.program_id(1) == 0)
    def _():
        xn_ref[...] = _rms(x_ref[...], g_ref[...]).astype(BF16)

    o_ref[...] = _dot(xn_ref[...], w_ref[...])


def _inproj(h, g_all, w_all, layer):
    t = h.shape[0]
    tm, tn = 1024, 1536
    return pl.pallas_call(
        _inproj_body,
        grid=(t // tm, IN_COLS // tn),
        in_specs=[
            pl.BlockSpec((tm, D_MODEL), lambda i, j: (i, 0)),
            pl.BlockSpec((None, 1, D_MODEL), lambda i, j: (layer, 0, 0)),
            pl.BlockSpec((None, D_MODEL, tn), lambda i, j: (layer, 0, j)),
        ],
        out_specs=pl.BlockSpec((tm, tn), lambda i, j: (i, j)),
        out_shape=jax.ShapeDtypeStruct((t, IN_COLS), F32),
        scratch_shapes=[pltpu.VMEM((tm, D_MODEL), BF16)],
        compiler_params=_params(("parallel", "arbitrary"), 40),
        name="inproj",
    )(h, g_all, w_all)


def _mix_prompt_body(u_ref, v_ref, q_ref, k_ref, vr_ref, gr_ref, xc_ref, cos_ref, sin_ref,
                     lng_ref, lnb_ref, ws_ref, bmap_ref, din_ref, dq_ref, dk_ref, dc_ref,
                     wp_ref, ps_ref,
                     a_ref, o_ref, c_ref, s_out_ref, buf_out_ref,
                     s_scr, xx_scr):
    n = pl.program_id(1)
    c = CHUNK
    hist = 2 * SUBLANES

    @pl.when(n == 0)
    def _():
        s_scr[...] = jnp.zeros_like(s_scr)
        xx_scr[0:hist, :] = jnp.zeros((hist, D_C), F32)

    row = lax.broadcasted_iota(jnp.int32, (c, c), 0)
    col = lax.broadcasted_iota(jnp.int32, (c, c), 1)

    u = jax.nn.gelu(u_ref[...])
    v = _layer_norm(jax.nn.gelu(v_ref[...]), lng_ref[...], lnb_ref[...])
    for g in range(G_A):
        sl = slice(g * DG_A, (g + 1) * DG_A)
        w = jnp.where(col <= row, ws_ref[g], 0.0).astype(BF16)
        mixed = _dot(w, v[:, sl].astype(BF16)) + bmap_ref[:, sl]
        a_ref[:, sl] = (u[:, sl] * mixed).astype(BF16)

    cos2 = cos_ref[...]
    sin2 = sin_ref[...]
    for h in range(H_RET):
        qs = slice(h * DK_RET, (h + 1) * DK_RET)
        vs = slice(h * DV_RET, (h + 1) * DV_RET)
        qr = _rotary(q_ref[:, qs], cos2, sin2)
        kr = _rotary(k_ref[:, qs], cos2, sin2) * (DK_RET ** -0.5)
        qb = qr.astype(BF16)
        vb = vr_ref[:, vs].astype(BF16)
        scores = _dot_nt(qb, kr.astype(BF16)) * din_ref[h]
        inner = _dot(scores.astype(BF16), vb)
        s_old = s_scr[h]
        cross = _dot(qb, s_old.astype(BF16)) * dq_ref[h]
        kd = (kr * dk_ref[h]).astype(BF16)
        s_scr[h] = dc_ref[h] * s_old + _dot_tn(kd, vb)
        oh = inner + cross
        oh = oh * lax.rsqrt(jnp.mean(oh * oh, axis=-1, keepdims=True) + EPS)
        o_ref[:, vs] = (jax.nn.silu(gr_ref[:, vs]) * oh).astype(BF16)

    xc = xc_ref[...]
    xx_scr[hist:hist + c, :] = xc
    pos1 = n * c + 1 + lax.broadcasted_iota(jnp.int32, (c, DG_C), 0)
    for gi, w in enumerate(POOL_WINDOWS):
        sl = slice(gi * DG_C, (gi + 1) * DG_C)
        acc = xx_scr[hist:hist + c, sl]
        for j in range(1, w):
            acc = acc + xx_scr[hist - j:hist - j + c, sl]
        cnt = jnp.minimum(w, pos1).astype(F32)
        pooled = acc / cnt - xc[:, sl]
        c_ref[:, sl] = (_dot(pooled.astype(BF16), wp_ref[gi]) * ps_ref[:, sl]).astype(BF16)
    xx_scr[0:hist, :] = xx_scr[c:c + hist, :]

    @pl.when(n == pl.num_programs(1) - 1)
    def _():
        s_out_ref[...] = s_scr[...]
        buf_out_ref[...] = xx_scr[hist - POOL_BUF:hist, :]


def _mix_prompt(z, batch, seq, tabs, lw, layer):
    c = CHUNK
    nc = seq // c
    rows = batch * seq

    def zspec(width, blk):
        return pl.BlockSpec((c, width), lambda b, n: (b * nc + n, blk))

    def whole(shape):
        nd = len(shape)
        return pl.BlockSpec((None,) + shape, lambda b, n: (layer,) + (0,) * nd)

    def const(shape):
        nd = len(shape)
        return pl.BlockSpec(shape, lambda b, n: (0,) * nd)

    in_specs = [
        zspec(D_A, BLK_U), zspec(D_A, BLK_V), zspec(QK_W, BLK_Q), zspec(QK_W, BLK_K),
        zspec(V_W, BLK_VR), zspec(V_W, BLK_GR), zspec(D_C, BLK_XC),
        pl.BlockSpec((c, DK_RET), lambda b, n: (n, 0)),
        pl.BlockSpec((c, DK_RET), lambda b, n: (n, 0)),
        whole((1, D_A)), whole((1, D_A)), whole((G_A, c, c)), whole((c, D_A)),
        const((H_RET, c, c)), const((H_RET, c, 1)), const((H_RET, c, 1)), const((H_RET, 1, 1)),
        whole((len(POOL_WINDOWS), DG_C, DG_C)), whole((1, D_C)),
    ]
    out_specs = [
        pl.BlockSpec((c, D_A), lambda b, n: (b * nc + n, 0)),
        pl.BlockSpec((c, V_W), lambda b, n: (b * nc + n, 0)),
        pl.BlockSpec((c, D_C), lambda b, n: (b * nc + n, 0)),
        pl.BlockSpec((None, H_RET, DK_RET, DV_RET), lambda b, n: (b, 0, 0, 0)),
        pl.BlockSpec((None, POOL_BUF, D_C), lambda b, n: (b, 0, 0)),
    ]
    out_shape = [
        jax.ShapeDtypeStruct((rows, D_A), BF16),
        jax.ShapeDtypeStruct((rows, V_W), BF16),
        jax.ShapeDtypeStruct((rows, D_C), BF16),
        jax.ShapeDtypeStruct((batch, H_RET, DK_RET, DV_RET), F32),
        jax.ShapeDtypeStruct((batch, POOL_BUF, D_C), F32),
    ]
    return pl.pallas_call(
        _mix_prompt_body,
        grid=(batch, nc),
        in_specs=in_specs,
        out_specs=out_specs,
        out_shape=out_shape,
        scratch_shapes=[pltpu.VMEM((H_RET, DK_RET, DV_RET), F32),
                        pltpu.VMEM((c + 2 * SUBLANES, D_C), F32)],
        compiler_params=_params(("arbitrary", "arbitrary"), 32),
        name="mix_prompt",
    )(z, z, z, z, z, z, z, tabs["cos"], tabs["sin"],
      lw["ln_g"], lw["ln_b"], lw["w_spatial"], lw["bmap_p"],
      tabs["din"], tabs["dq"], tabs["dk"], tabs["dc"],
      lw["w_pool"], lw["pool_scale"])


SAMPLE_NB = 8


def _mix_sample_body(u_ref, v_ref, q_ref, k_ref, vr_ref, gr_ref, xc_ref, cos_ref, sin_ref,
                     lng_ref, lnb_ref, wmix_ref, bmap_ref, din_ref, dq_ref, dk_ref, dc_ref,
                     wp_ref, ps_ref, s_in_ref, buf_in_ref,
                     a_ref, o_ref, c_ref, vn_ref, s_out_ref, buf_out_ref,
                     xx_scr, *, seq, pos0):
    nb = SAMPLE_NB
    r = nb * seq
    hist = 2 * SUBLANES

    u = jax.nn.gelu(u_ref[...])
    v = _layer_norm(jax.nn.gelu(v_ref[...]), lng_ref[...], lnb_ref[...])
    vn_ref[...] = v
    v3 = v.reshape(nb, seq, D_A)
    t_idx = lax.broadcasted_iota(jnp.int32, (seq, D_A), 0)
    mixed = jnp.broadcast_to(bmap_ref[...][None], (nb, seq, D_A))
    for s in range(seq):
        w_s = jnp.where(t_idx >= s, wmix_ref[s], 0.0)
        mixed = mixed + w_s[None] * v3[:, s:s + 1, :]
    a_ref[...] = (u * mixed.reshape(r, D_A)).astype(BF16)

    cos2 = cos_ref[...]
    sin2 = sin_ref[...]
    rowb = lax.broadcasted_iota(jnp.int32, (r, DV_RET), 0) // seq
    for h in range(H_RET):
        qs = slice(h * DK_RET, (h + 1) * DK_RET)
        vs = slice(h * DV_RET, (h + 1) * DV_RET)
        qr = _rotary(q_ref[:, qs], cos2, sin2)
        kr = _rotary(k_ref[:, qs], cos2, sin2) * (DK_RET ** -0.5)
        qb = qr.astype(BF16)
        vh = vr_ref[:, vs]
        scores = _dot_nt(qb, kr.astype(BF16)) * din_ref[h]
        inner = _dot(scores.astype(BF16), vh.astype(BF16))
        kd = (kr * dk_ref[h]).astype(BF16)
        cross = jnp.zeros((r, DV_RET), F32)
        for b in range(nb):
            s_old = s_in_ref[b, h]
            cross = jnp.where(rowb == b, _dot(qb, s_old.astype(BF16)), cross)
            vm = jnp.where(rowb == b, vh, 0.0).astype(BF16)
            s_out_ref[b, h] = dc_ref[h] * s_old + _dot_tn(kd, vm)
        oh = inner + cross * dq_ref[h]
        oh = oh * lax.rsqrt(jnp.mean(oh * oh, axis=-1, keepdims=True) + EPS)
        o_ref[:, vs] = (jax.nn.silu(gr_ref[:, vs]) * oh).astype(BF16)

    xc = xc_ref[...]
    xx_scr[:, hist - POOL_BUF:hist, :] = buf_in_ref[...]
    xx_scr[:, hist:hist + seq, :] = xc.reshape(nb, seq, D_C)
    pos1 = pos0 + 1 + lax.broadcasted_iota(jnp.int32, (seq, DG_C), 0)
    for gi, w in enumerate(POOL_WINDOWS):
        sl = slice(gi * DG_C, (gi + 1) * DG_C)
        acc = xx_scr[:, hist:hist + seq, sl]
        for j in range(1, w):
            acc = acc + xx_scr[:, hist - j:hist - j + seq, sl]
        cnt = jnp.minimum(w, pos1).astype(F32)
        pooled = (acc / cnt[None]).reshape(r, DG_C) - xc[:, sl]
        c_ref[:, sl] = (_dot(pooled.astype(BF16), wp_ref[gi]) * ps_ref[:, sl]).astype(BF16)
    buf_out_ref[...] = xx_scr[:, hist + seq - POOL_BUF:hist + seq, :]


def _mix_sample(z, row0, batch, seq, pos0, tabs, lw, state_ret, state_pool, layer):
    nb = SAMPLE_NB
    r = nb * seq
    blk0 = row0 // r
    rows = batch * seq

    def zspec(width, blk):
        return pl.BlockSpec((r, width), lambda i: (blk0 + i, blk))

    def whole(shape):
        nd = len(shape)
        return pl.BlockSpec((None,) + shape, lambda i: (layer,) + (0,) * nd)

    def const(shape):
        nd = len(shape)
        return pl.BlockSpec(shape, lambda i: (0,) * nd)

    in_specs = [
        zspec(D_A, BLK_U), zspec(D_A, BLK_V), zspec(QK_W, BLK_Q), zspec(QK_W, BLK_K),
        zspec(V_W, BLK_VR), zspec(V_W, BLK_GR), zspec(D_C, BLK_XC),
        const((r, DK_RET)), const((r, DK_RET)),
        whole((1, D_A)), whole((1, D_A)), whole((seq, seq, D_A)), whole((seq, D_A)),
        const((H_RET, r, r)), const((H_RET, r, 1)), const((H_RET, r, 1)), const((H_RET, 1, 1)),
        whole((len(POOL_WINDOWS), DG_C, DG_C)), whole((1, D_C)),
        pl.BlockSpec((None, nb, H_RET, DK_RET, DV_RET), lambda i: (layer, i, 0, 0, 0)),
        pl.BlockSpec((None, nb, POOL_BUF, D_C), lambda i: (layer, i, 0, 0)),
    ]
    out_specs = [
        pl.BlockSpec((r, D_A), lambda i: (i, 0)),
        pl.BlockSpec((r, V_W), lambda i: (i, 0)),
        pl.BlockSpec((r, D_C), lambda i: (i, 0)),
        pl.BlockSpec((r, D_A), lambda i: (i, 0)),
        pl.BlockSpec((nb, H_RET, DK_RET, DV_RET), lambda i: (i, 0, 0, 0)),
        pl.BlockSpec((nb, POOL_BUF, D_C), lambda i: (i, 0, 0)),
    ]
    out_shape = [
        jax.ShapeDtypeStruct((rows, D_A), BF16),
        jax.ShapeDtypeStruct((rows, V_W), BF16),
        jax.ShapeDtypeStruct((rows, D_C), BF16),
        jax.ShapeDtypeStruct((rows, D_A), F32),
        jax.ShapeDtypeStruct((batch, H_RET, DK_RET, DV_RET), F32),
        jax.ShapeDtypeStruct((batch, POOL_BUF, D_C), F32),
    ]
    return pl.pallas_call(
        functools.partial(_mix_sample_body, seq=seq, pos0=pos0),
        grid=(batch // nb,),
        in_specs=in_specs,
        out_specs=out_specs,
        out_shape=out_shape,
        scratch_shapes=[pltpu.VMEM((nb, 2 * SUBLANES + seq, D_C), F32)],
        compiler_params=_params(("parallel",), 40),
        name="mix_sample",
    )(z, z, z, z, z, z, z, tabs["cos"], tabs["sin"],
      lw["ln_g"], lw["ln_b"], lw["wmix_s"], lw["bmap_s"],
      tabs["din"], tabs["dq"], tabs["dk"], tabs["dc"],
      lw["w_pool"], lw["pool_scale"], state_ret, state_pool)


def _merge_body(a_ref, o_ref, c_ref, ga_ref, gb_ref, gc_ref, h_ref,
                wa_ref, wb_ref, wc_ref, wo_ref, out_ref):
    m = jax.nn.sigmoid(ga_ref[...]) * _dot(a_ref[...], wa_ref[...])
    m = m + jax.nn.sigmoid(gb_ref[...]) * _dot(o_ref[...], wb_ref[...])
    m = m + jax.nn.sigmoid(gc_ref[...]) * _dot(c_ref[...], wc_ref[...])
    out_ref[...] = h_ref[...] + _dot(m.astype(BF16), wo_ref[...])


def _merge(a, o, c, z, h, lw, layer):
    t = h.shape[0]
    tm = 512

    def rows(width, blk=0):
        return pl.BlockSpec((tm, width), lambda i: (i, blk))

    def whole(shape):
        nd = len(shape)
        return pl.BlockSpec((None,) + shape, lambda i: (layer,) + (0,) * nd)

    return pl.pallas_call(
        _merge_body,
        grid=(t // tm,),
        in_specs=[rows(D_A), rows(V_W), rows(D_C),
                  rows(D_MODEL, BLK_GA), rows(D_MODEL, BLK_GB), rows(D_MODEL, BLK_GC),
                  rows(D_MODEL),
                  whole((D_A, D_MODEL)), whole((V_W, D_MODEL)), whole((D_C, D_MODEL)),
                  whole((D_MODEL, D_MODEL))],
        out_specs=rows(D_MODEL),
        out_shape=jax.ShapeDtypeStruct((t, D_MODEL), F32),
        compiler_params=_params(("parallel",), 48),
        name="merge",
    )(a, o, c, z, z, z, h, lw["w_br_a"], lw["w_br_b"], lw["w_br_c"], lw["w_out"])


def _ffn_body(x_ref, g_ref, wg_ref, wu_ref, wd_ref, out_ref, xn_ref, acc_ref):
    j = pl.program_id(1)

    @pl.when(j == 0)
    def _():
        xn_ref[...] = _rms(x_ref[...], g_ref[...]).astype(BF16)
        acc_ref[...] = jnp.zeros_like(acc_ref)

    xn = xn_ref[...]
    hid = jax.nn.silu(_dot(xn, wg_ref[...])) * _dot(xn, wu_ref[...])
    acc_ref[...] += _dot(hid.astype(BF16), wd_ref[...])

    @pl.when(j == pl.num_programs(1) - 1)
    def _():
        out_ref[...] = x_ref[...] + acc_ref[...]


def _ffn_dense(h, g_all, wg, wu, wd, layer, j):
    t = h.shape[0]
    tm, tf = 512, 1408
    return pl.pallas_call(
        _ffn_body,
        grid=(t // tm, D_FF // tf),
        in_specs=[
            pl.BlockSpec((tm, D_MODEL), lambda i, f: (i, 0)),
            pl.BlockSpec((None, 1, D_MODEL), lambda i, f: (layer, 0, 0)),
            pl.BlockSpec((None, D_MODEL, tf), lambda i, f: (j, 0, f)),
            pl.BlockSpec((None, D_MODEL, tf), lambda i, f: (j, 0, f)),
            pl.BlockSpec((None, tf, D_MODEL), lambda i, f: (j, f, 0)),
        ],
        out_specs=pl.BlockSpec((tm, D_MODEL), lambda i, f: (i, 0)),
        out_shape=jax.ShapeDtypeStruct((t, D_MODEL), F32),
        scratch_shapes=[pltpu.VMEM((tm, D_MODEL), BF16), pltpu.VMEM((tm, D_MODEL), F32)],
        compiler_params=_params(("parallel", "arbitrary"), 48),
        name="ffn_dense",
    )(h, g_all, wg, wu, wd)


def _router_body(x_ref, g_ref, wr_ref, cmb_ref):
    xn = _rms(x_ref[...], g_ref[...])
    logits = jnp.dot(xn, wr_ref[...], preferred_element_type=F32, precision=lax.Precision.HIGHEST)
    lane = lax.broadcasted_iota(jnp.int32, logits.shape, 1)
    neg = jnp.float32(-jnp.inf)
    logits = jnp.where(lane < N_EXPERTS, logits, neg)
    v1 = jnp.max(logits, axis=-1, keepdims=True)
    i1 = jnp.min(jnp.where(logits == v1, lane, LANES), axis=-1, keepdims=True)
    rest = jnp.where(lane == i1, neg, logits)
    v2 = jnp.max(rest, axis=-1, keepdims=True)
    i2 = jnp.min(jnp.where(rest == v2, lane, LANES), axis=-1, keepdims=True)
    e2 = jnp.exp(v2 - v1)
    den = 1.0 + e2
    cmb_ref[...] = jnp.where(lane == i1, 1.0 / den, 0.0) + jnp.where(lane == i2, e2 / den, 0.0)


def _router(h, g_all, wr_all, layer, j):
    t = h.shape[0]
    tm = 512
    return pl.pallas_call(
        _router_body,
        grid=(t // tm,),
        in_specs=[
            pl.BlockSpec((tm, D_MODEL), lambda i: (i, 0)),
            pl.BlockSpec((None, 1, D_MODEL), lambda i: (layer, 0, 0)),
            pl.BlockSpec((None, D_MODEL, LANES), lambda i: (j, 0, 0)),
        ],
        out_specs=pl.BlockSpec((tm, LANES), lambda i: (i, 0)),
        out_shape=jax.ShapeDtypeStruct((t, LANES), F32),
        compiler_params=_params(("parallel",), 32),
        name="router",
    )(h, g_all, wr_all)


def _moe_body(x_ref, g_ref, cmb_ref, wg_ref, wu_ref, wd_ref, out_ref, xn_ref, acc_ref):
    e = pl.program_id(1)
    f = pl.program_id(2)

    @pl.when((e == 0) & (f == 0))
    def _():
        xn_ref[...] = _rms(x_ref[...], g_ref[...]).astype(BF16)
        acc_ref[...] = jnp.zeros_like(acc_ref)

    cmb = cmb_ref[...]
    lane = lax.broadcasted_iota(jnp.int32, cmb.shape, 1)
    w_e = jnp.sum(jnp.where(lane == e, cmb, 0.0), axis=-1, keepdims=True)
    xn = xn_ref[...]
    hid = jax.nn.silu(_dot(xn, wg_ref[...])) * _dot(xn, wu_ref[...])
    acc_ref[...] += _dot((w_e * hid).astype(BF16), wd_ref[...])

    @pl.when((e == pl.num_programs(1) - 1) & (f == pl.num_programs(2) - 1))
    def _():
        out_ref[...] = x_ref[...] + acc_ref[...]


def _ffn_moe(h, g_all, cmb, wg, wu, wd, layer, j):
    t = h.shape[0]
    tm, tf = 512, 512
    return pl.pallas_call(
        _moe_body,
        grid=(t // tm, N_EXPERTS, D_FF_EXP // tf),
        in_specs=[
            pl.BlockSpec((tm, D_MODEL), lambda i, e, f: (i, 0)),
            pl.BlockSpec((None, 1, D_MODEL), lambda i, e, f: (layer, 0, 0)),
            pl.BlockSpec((tm, LANES), lambda i, e, f: (i, 0)),
            pl.BlockSpec((None, None, D_MODEL, tf), lambda i, e, f: (j, e, 0, f)),
            pl.BlockSpec((None, None, D_MODEL, tf), lambda i, e, f: (j, e, 0, f)),
            pl.BlockSpec((None, None, tf, D_MODEL), lambda i, e, f: (j, e, f, 0)),
        ],
        out_specs=pl.BlockSpec((tm, D_MODEL), lambda i, e, f: (i, 0)),
        out_shape=jax.ShapeDtypeStruct((t, D_MODEL), F32),
        scratch_shapes=[pltpu.VMEM((tm, D_MODEL), BF16), pltpu.VMEM((tm, D_MODEL), F32)],
        compiler_params=_params(("parallel", "arbitrary", "arbitrary"), 32),
        name="ffn_moe",
    )(h, g_all, cmb, wg, wu, wd)


def _final_norm_body(x_ref, g_ref, o_ref):
    o_ref[...] = _rms(x_ref[...], g_ref[...])


def _final_norm(h, g):
    t = h.shape[0]
    tm = 1024
    return pl.pallas_call(
        _final_norm_body,
        grid=(t // tm,),
        in_specs=[pl.BlockSpec((tm, D_MODEL), lambda i: (i, 0)),
                  pl.BlockSpec((1, D_MODEL), lambda i: (0, 0))],
        out_specs=pl.BlockSpec((tm, D_MODEL), lambda i: (i, 0)),
        out_shape=jax.ShapeDtypeStruct((t, D_MODEL), F32),
        compiler_params=_params(("parallel",), 32),
        name="final_norm",
    )(h, g)


def _rope_tables(pos):
    half = DK_RET // 2
    inv = ROPE_BASE ** (-jnp.arange(half, dtype=F32) / half)
    ang = pos.astype(F32)[:, None] * inv[None, :]
    cos, sin = jnp.cos(ang), jnp.sin(ang)
    return jnp.concatenate([cos, cos], axis=-1), jnp.concatenate([-sin, sin], axis=-1)


def _decay_tables(c):
    log_g = jnp.log1p(-jnp.exp2(-5.0 - jnp.arange(H_RET, dtype=F32)))
    idx = jnp.arange(c, dtype=F32)
    diff = idx[:, None] - idx[None, :]
    causal = diff >= 0
    d_in = jnp.where(causal[None], jnp.exp(log_g[:, None, None] * jnp.where(causal, diff, 0.0)[None]), 0.0)
    d_q = jnp.exp(log_g[:, None] * (idx[None, :] + 1.0))
    d_k = jnp.exp(log_g[:, None] * (c - 1.0 - idx[None, :]))
    d_c = jnp.exp(log_g * c)
    return d_in, d_q, d_k, d_c


def _prompt_tables(seq):
    c = min(CHUNK, seq)
    cos2, sin2 = _rope_tables(jnp.arange(seq, dtype=jnp.int32))
    d_in, d_q, d_k, d_c = _decay_tables(c)
    return dict(cos=cos2, sin=sin2, din=d_in, dq=d_q[:, :, None], dk=d_k[:, :, None],
                dc=d_c[:, None, None])


def _sample_tables(seq, pos0):
    nb = SAMPLE_NB
    cos2, sin2 = _rope_tables(pos0 + jnp.arange(seq, dtype=jnp.int32))
    d_in, d_q, d_k, d_c = _decay_tables(seq)
    eye = jnp.eye(nb, dtype=F32)
    d_blk = jnp.einsum("ab,hij->haibj", eye, d_in).reshape(H_RET, nb * seq, nb * seq)
    return dict(cos=jnp.tile(cos2, (nb, 1)), sin=jnp.tile(sin2, (nb, 1)), din=d_blk,
                dq=jnp.tile(d_q, (1, nb))[:, :, None], dk=jnp.tile(d_k, (1, nb))[:, :, None],
                dc=d_c[:, None, None])


def _permute_in_cols(w_in):
    parts = [w_in[..., _REF_OFF[k]:_REF_OFF[k] + _REF_W[k]] for k in _NEW_ORDER]
    return jnp.concatenate(parts, axis=-1)


def kernel(x_prompt, x_sample, state_ret, state_pool, norm1_g, w_in, ln_v_g, ln_v_b, w_spatial, b_spatial, w_pool, pool_scale, w_br_a, w_br_b, w_br_c, w_out, norm2_g, w_ffn_gate, w_ffn_up, w_ffn_down, w_router, w_exp_gate, w_exp_up, w_exp_down, final_norm_g):
    batch, seq, _ = x_prompt.shape
    dec_batch, dec_seq, _ = x_sample.shape
    n_prompt = batch * seq
    assert seq % CHUNK == 0 and dec_seq <= CHUNK and PAST_LEN % CHUNK == 0
    assert n_prompt % (SAMPLE_NB * dec_seq) == 0 and dec_batch % SAMPLE_NB == 0

    w_in_p = _permute_in_cols(w_in).astype(BF16)
    lw = dict(
        ln_g=ln_v_g[:, None, :], ln_b=ln_v_b[:, None, :],
        w_spatial=w_spatial,
        bmap_p=jnp.repeat(jnp.swapaxes(b_spatial, 1, 2), DG_A, axis=2),
        wmix_s=jnp.repeat(jnp.transpose(w_spatial[:, :, :dec_seq, :dec_seq], (0, 3, 2, 1)), DG_A, axis=3),
        w_pool=w_pool.astype(BF16), pool_scale=pool_scale[:, None, :],
        w_br_a=w_br_a.astype(BF16), w_br_b=w_br_b.astype(BF16), w_br_c=w_br_c.astype(BF16),
        w_out=w_out.astype(BF16),
    )
    lw["bmap_s"] = lw["bmap_p"][:, :dec_seq, :]
    g1 = norm1_g[:, None, :]
    g2 = norm2_g[:, None, :]
    wfg, wfu, wfd = w_ffn_gate.astype(BF16), w_ffn_up.astype(BF16), w_ffn_down.astype(BF16)
    weg, weu, wed = w_exp_gate.astype(BF16), w_exp_up.astype(BF16), w_exp_down.astype(BF16)
    wr = jnp.pad(w_router, ((0, 0), (0, 0), (0, LANES - N_EXPERTS)))
    tabs_p = _prompt_tables(seq)
    tabs_s = _sample_tables(dec_seq, PAST_LEN)

    h = jnp.concatenate([x_prompt.reshape(n_prompt, D_MODEL),
                         x_sample.reshape(dec_batch * dec_seq, D_MODEL)], axis=0)
    ret_p, ret_s, pool_p, pool_s, v_s = [], [], [], [], []
    for layer in range(DEPTH):
        j = layer // 2
        z = _inproj(h, g1, w_in_p, layer)
        a_p, o_p, c_p, s_p, b_p = _mix_prompt(z, batch, seq, tabs_p, lw, layer)
        a_s, o_s, c_s, vn, s_s, b_s = _mix_sample(z, n_prompt, dec_batch, dec_seq, PAST_LEN,
                                                 tabs_s, lw, state_ret, state_pool, layer)
        a = jnp.concatenate([a_p, a_s], axis=0)
        o = jnp.concatenate([o_p, o_s], axis=0)
        c = jnp.concatenate([c_p, c_s], axis=0)
        h = _merge(a, o, c, z, h, lw, layer)
        if layer % 2 == 0:
            h = _ffn_dense(h, g2, wfg, wfu, wfd, layer, j)
        else:
            cmb = _router(h, g2, wr, layer, j)
            h = _ffn_moe(h, g2, cmb, weg, weu, wed, layer, j)
        ret_p.append(s_p)
        ret_s.append(s_s)
        pool_p.append(b_p)
        pool_s.append(b_s)
        v_s.append(vn.reshape(dec_batch, dec_seq, D_A))

    y = _final_norm(h, final_norm_g[None, :])
    y_prompt = y[:n_prompt].reshape(batch, seq, D_MODEL)
    y_sample = y[n_prompt:].reshape(dec_batch, dec_seq, D_MODEL)
    return (y_prompt, y_sample, jnp.stack(ret_p), jnp.stack(ret_s), jnp.stack(pool_p),
            jnp.stack(pool_s), jnp.stack(v_s))
```

```python
import functools

import jax
import jax.numpy as jnp
from jax import lax
from jax.experimental import pallas as pl
from jax.experimental.pallas import tpu as pltpu

F32 = jnp.float32
BF16 = jnp.bfloat16
I32 = jnp.int32

D_MODEL = 1024
DEPTH = 4
PAST_LEN = 16384
D_A = D_MODEL // 2
CHUNK = 128
G_A = 4
DG_A = D_A // G_A
H_RET = 4
DK_RET = D_MODEL // 2 // H_RET
DV_RET = 2 * DK_RET
ROPE_BASE = 10000.0
D_C = D_MODEL // 2
POOL_WINDOWS = (2, 4, 8, 16)
DG_C = D_C // len(POOL_WINDOWS)
POOL_BUF = max(POOL_WINDOWS) - 1
D_FF = 2816
N_EXPERTS = 8
TOP_K = 2
D_FF_EXP = 3584
EPS = 1e-6
QK_W = H_RET * DK_RET
V_W = H_RET * DV_RET
IN_COLS = 2 * D_A + 2 * QK_W + 2 * V_W + D_C + 3 * D_MODEL

LANES = 128
SUBLANES = 8
MIB = 1024 * 1024

BLK_U, BLK_V, BLK_Q, BLK_K, BLK_XC = 0, 1, 2, 3, 8
BLK_VR, BLK_GR = 2, 3
BLK_GA, BLK_GB, BLK_GC = 9, 11, 13
HALF = D_MODEL // 2

MOE_TILE = 1024
ROW_TILE = 512


def _params(sem, vmem_mib):
    return pltpu.CompilerParams(dimension_semantics=sem, vmem_limit_bytes=vmem_mib * MIB)


def _rms(x, g):
    return x * lax.rsqrt(jnp.mean(x * x, axis=-1, keepdims=True) + EPS) * g


def _layer_norm(x, g, b):
    mu = jnp.mean(x, axis=-1, keepdims=True)
    xc = x - mu
    var = jnp.mean(xc * xc, axis=-1, keepdims=True)
    return xc * lax.rsqrt(var + EPS) * g + b


def _dot(a, b):
    return jnp.dot(a, b, preferred_element_type=F32)


def _dot_nt(a, b):
    return lax.dot_general(a, b, (((1,), (1,)), ((), ())), preferred_element_type=F32)


def _dot_tn(a, b):
    return lax.dot_general(a, b, (((0,), (0,)), ((), ())), preferred_element_type=F32)


def _rotary(x, cos2, sin2):
    return x * cos2 + pltpu.roll(x, DK_RET // 2, 1) * sin2


def _any_spec():
    return pl.BlockSpec(memory_space=pl.ANY)


def _inproj_body(x_ref, g_ref, w_ref, o_ref, xn_ref):
    @pl.when(pl.program_id(1) == 0)
    def _():
        xn_ref[...] = _rms(x_ref[...], g_ref[...]).astype(BF16)

    o_ref[...] = _dot(xn_ref[...], w_ref[...].astype(BF16))


def _inproj(h, g_all, w_all, layer):
    t = h.shape[0]
    tm, tn = 1024, 1536
    return pl.pallas_call(
        _inproj_body,
        grid=(t // tm, IN_COLS // tn),
        in_specs=[
            pl.BlockSpec((tm, D_MODEL), lambda i, j: (i, 0)),
            pl.BlockSpec((None, 1, D_MODEL), lambda i, j: (layer, 0, 0)),
            pl.BlockSpec((None, D_MODEL, tn), lambda i, j: (layer, 0, j)),
        ],
        out_specs=pl.BlockSpec((tm, tn), lambda i, j: (i, j)),
        out_shape=jax.ShapeDtypeStruct((t, IN_COLS), F32),
        scratch_shapes=[pltpu.VMEM((tm, D_MODEL), BF16)],
        compiler_params=_params(("parallel", "arbitrary"), 48),
        name="inproj",
    )(h, g_all, w_all)


def _mix_prompt_body(u_ref, v_ref, q_ref, k_ref, vr_ref, gr_ref, xc_ref, cos_ref, sin_ref,
                     lng_ref, lnb_ref, ws_ref, bmap_ref, din_ref, dq_ref, dk_ref, dc_ref,
                     wp_ref, ps_ref,
                     a_ref, o_ref, c_ref, s_out_ref, buf_out_ref,
                     s_scr, xx_scr):
    n = pl.program_id(1)
    c = CHUNK
    hist = 2 * SUBLANES

    @pl.when(n == 0)
    def _():
        s_scr[...] = jnp.zeros_like(s_scr)
        xx_scr[0:hist, :] = jnp.zeros((hist, D_C), F32)

    row = lax.broadcasted_iota(I32, (c, c), 0)
    col = lax.broadcasted_iota(I32, (c, c), 1)

    u = jax.nn.gelu(u_ref[...])
    v = _layer_norm(jax.nn.gelu(v_ref[...]), lng_ref[...], lnb_ref[...])
    for g in range(G_A):
        sl = slice(g * DG_A, (g + 1) * DG_A)
        w = jnp.where(col <= row, ws_ref[g], 0.0).astype(BF16)
        mixed = _dot(w, v[:, sl].astype(BF16)) + bmap_ref[:, sl]
        a_ref[:, sl] = (u[:, sl] * mixed).astype(BF16)

    cos2 = cos_ref[...]
    sin2 = sin_ref[...]
    for h in range(H_RET):
        qs = slice(h * DK_RET, (h + 1) * DK_RET)
        vs = slice(h * DV_RET, (h + 1) * DV_RET)
        qr = _rotary(q_ref[:, qs], cos2, sin2)
        kr = _rotary(k_ref[:, qs], cos2, sin2) * (DK_RET ** -0.5)
        qb = qr.astype(BF16)
        vb = vr_ref[:, vs].astype(BF16)
        scores = _dot_nt(qb, kr.astype(BF16)) * din_ref[h]
        inner = _dot(scores.astype(BF16), vb)
        s_old = s_scr[h]
        cross = _dot(qb, s_old.astype(BF16)) * dq_ref[h]
        kd = (kr * dk_ref[h]).astype(BF16)
        s_scr[h] = dc_ref[h] * s_old + _dot_tn(kd, vb)
        oh = inner + cross
        oh = oh * lax.rsqrt(jnp.mean(oh * oh, axis=-1, keepdims=True) + EPS)
        o_ref[:, vs] = (jax.nn.silu(gr_ref[:, vs]) * oh).astype(BF16)

    xc = xc_ref[...]
    xx_scr[hist:hist + c, :] = xc
    pos1 = n * c + 1 + lax.broadcasted_iota(I32, (c, DG_C), 0)
    for gi, w in enumerate(POOL_WINDOWS):
        sl = slice(gi * DG_C, (gi + 1) * DG_C)
        acc = xx_scr[hist:hist + c, sl]
        for j in range(1, w):
            acc = acc + xx_scr[hist - j:hist - j + c, sl]
        cnt = jnp.minimum(w, pos1).astype(F32)
        pooled = acc / cnt - xc[:, sl]
        c_ref[:, sl] = (_dot(pooled.astype(BF16), wp_ref[gi]) * ps_ref[:, sl]).astype(BF16)
    xx_scr[0:hist, :] = xx_scr[c:c + hist, :]

    @pl.when(n == pl.num_programs(1) - 1)
    def _():
        s_out_ref[...] = s_scr[...]
        buf_out_ref[...] = xx_scr[hist - POOL_BUF:hist, :]


def _mix_prompt(z, batch, seq, tabs, lw, layer):
    c = CHUNK
    nc = seq // c
    t = z.shape[0]

    def zspec(width, blk):
        return pl.BlockSpec((c, width), lambda b, n: (b * nc + n, blk))

    def whole(shape):
        nd = len(shape)
        return pl.BlockSpec((None,) + shape, lambda b, n: (layer,) + (0,) * nd)

    def const(shape):
        nd = len(shape)
        return pl.BlockSpec(shape, lambda b, n: (0,) * nd)

    in_specs = [
        zspec(D_A, BLK_U), zspec(D_A, BLK_V), zspec(QK_W, BLK_Q), zspec(QK_W, BLK_K),
        zspec(V_W, BLK_VR), zspec(V_W, BLK_GR), zspec(D_C, BLK_XC),
        pl.BlockSpec((c, DK_RET), lambda b, n: (n, 0)),
        pl.BlockSpec((c, DK_RET), lambda b, n: (n, 0)),
        whole((1, D_A)), whole((1, D_A)), whole((G_A, c, c)), whole((c, D_A)),
        const((H_RET, c, c)), const((H_RET, c, 1)), const((H_RET, c, 1)), const((H_RET, 1, 1)),
        whole((len(POOL_WINDOWS), DG_C, DG_C)), whole((1, D_C)),
    ]
    out_specs = [
        pl.BlockSpec((c, D_A), lambda b, n: (b * nc + n, 0)),
        pl.BlockSpec((c, V_W), lambda b, n: (b * nc + n, 0)),
        pl.BlockSpec((c, D_C), lambda b, n: (b * nc + n, 0)),
        pl.BlockSpec((None, H_RET, DK_RET, DV_RET), lambda b, n: (b, 0, 0, 0)),
        pl.BlockSpec((None, POOL_BUF, D_C), lambda b, n: (b, 0, 0)),
    ]
    out_shape = [
        jax.ShapeDtypeStruct((t, D_A), BF16),
        jax.ShapeDtypeStruct((t, V_W), BF16),
        jax.ShapeDtypeStruct((t, D_C), BF16),
        jax.ShapeDtypeStruct((batch, H_RET, DK_RET, DV_RET), F32),
        jax.ShapeDtypeStruct((batch, POOL_BUF, D_C), F32),
    ]
    return pl.pallas_call(
        _mix_prompt_body,
        grid=(batch, nc),
        in_specs=in_specs,
        out_specs=out_specs,
        out_shape=out_shape,
        scratch_shapes=[pltpu.VMEM((H_RET, DK_RET, DV_RET), F32),
                        pltpu.VMEM((c + 2 * SUBLANES, D_C), F32)],
        compiler_params=_params(("arbitrary", "arbitrary"), 32),
        name="mix_prompt",
    )(z, z, z, z, z, z, z, tabs["cos"], tabs["sin"],
      lw["ln_g"], lw["ln_b"], lw["w_spatial"], lw["bmap_p"],
      tabs["din"], tabs["dq"], tabs["dk"], tabs["dc"],
      lw["w_pool"], lw["pool_scale"])


SAMPLE_NB = 8


def _mix_sample_body(u_ref, v_ref, q_ref, k_ref, vr_ref, gr_ref, xc_ref, cos_ref, sin_ref,
                     lng_ref, lnb_ref, wmix_ref, bmap_ref, din_ref, dq_ref, dk_ref, dc_ref,
                     wp_ref, ps_ref, s_in_ref, buf_in_ref, *rest, seq, pos0, n_alias):
    a_ref, o_ref, c_ref, vn_ref, s_out_ref, buf_out_ref, xx_scr = rest[n_alias:]
    nb = SAMPLE_NB
    r = nb * seq
    hist = 2 * SUBLANES

    u = jax.nn.gelu(u_ref[...])
    v = _layer_norm(jax.nn.gelu(v_ref[...]), lng_ref[...], lnb_ref[...])
    vn_ref[...] = v
    v3 = v.reshape(nb, seq, D_A)
    t_idx = lax.broadcasted_iota(I32, (seq, D_A), 0)
    mixed = jnp.broadcast_to(bmap_ref[...][None], (nb, seq, D_A))
    for s in range(seq):
        w_s = jnp.where(t_idx >= s, wmix_ref[s], 0.0)
        mixed = mixed + w_s[None] * v3[:, s:s + 1, :]
    a_ref[...] = (u * mixed.reshape(r, D_A)).astype(BF16)

    cos2 = cos_ref[...]
    sin2 = sin_ref[...]
    rowb = lax.broadcasted_iota(I32, (r, DV_RET), 0) // seq
    for h in range(H_RET):
        qs = slice(h * DK_RET, (h + 1) * DK_RET)
        vs = slice(h * DV_RET, (h + 1) * DV_RET)
        qr = _rotary(q_ref[:, qs], cos2, sin2)
        kr = _rotary(k_ref[:, qs], cos2, sin2) * (DK_RET ** -0.5)
        qb = qr.astype(BF16)
        vh = vr_ref[:, vs]
        scores = _dot_nt(qb, kr.astype(BF16)) * din_ref[h]
        inner = _dot(scores.astype(BF16), vh.astype(BF16))
        kd = (kr * dk_ref[h]).astype(BF16)
        cross = jnp.zeros((r, DV_RET), F32)
        for b in range(nb):
            s_old = s_in_ref[b, h]
            cross = jnp.where(rowb == b, _dot(qb, s_old.astype(BF16)), cross)
            vm = jnp.where(rowb == b, vh, 0.0).astype(BF16)
            s_out_ref[b, h] = dc_ref[h] * s_old + _dot_tn(kd, vm)
        oh = inner + cross * dq_ref[h]
        oh = oh * lax.rsqrt(jnp.mean(oh * oh, axis=-1, keepdims=True) + EPS)
        o_ref[:, vs] = (jax.nn.silu(gr_ref[:, vs]) * oh).astype(BF16)

    xc = xc_ref[...]
    xx_scr[:, hist - POOL_BUF:hist, :] = buf_in_ref[...]
    xx_scr[:, hist:hist + seq, :] = xc.reshape(nb, seq, D_C)
    pos1 = pos0 + 1 + lax.broadcasted_iota(I32, (seq, DG_C), 0)
    for gi, w in enumerate(POOL_WINDOWS):
        sl = slice(gi * DG_C, (gi + 1) * DG_C)
        acc = xx_scr[:, hist:hist + seq, sl]
        for j in range(1, w):
            acc = acc + xx_scr[:, hist - j:hist - j + seq, sl]
        cnt = jnp.minimum(w, pos1).astype(F32)
        pooled = (acc / cnt[None]).reshape(r, DG_C) - xc[:, sl]
        c_ref[:, sl] = (_dot(pooled.astype(BF16), wp_ref[gi]) * ps_ref[:, sl]).astype(BF16)
    buf_out_ref[...] = xx_scr[:, hist + seq - POOL_BUF:hist + seq, :]


def _mix_sample(z, a_buf, o_buf, c_buf, ret_buf, row0, batch, seq, pos0, tabs, lw,
                state_ret, state_pool, layer):
    nb = SAMPLE_NB
    r = nb * seq
    blk0 = row0 // r
    rows = batch * seq
    t = z.shape[0]

    def zspec(width, blk):
        return pl.BlockSpec((r, width), lambda i: (blk0 + i, blk))

    def whole(shape):
        nd = len(shape)
        return pl.BlockSpec((None,) + shape, lambda i: (layer,) + (0,) * nd)

    def const(shape):
        nd = len(shape)
        return pl.BlockSpec(shape, lambda i: (0,) * nd)

    in_specs = [
        zspec(D_A, BLK_U), zspec(D_A, BLK_V), zspec(QK_W, BLK_Q), zspec(QK_W, BLK_K),
        zspec(V_W, BLK_VR), zspec(V_W, BLK_GR), zspec(D_C, BLK_XC),
        const((r, DK_RET)), const((r, DK_RET)),
        whole((1, D_A)), whole((1, D_A)), whole((seq, seq, D_A)), whole((seq, D_A)),
        const((H_RET, r, r)), const((H_RET, r, 1)), const((H_RET, r, 1)), const((H_RET, 1, 1)),
        whole((len(POOL_WINDOWS), DG_C, DG_C)), whole((1, D_C)),
        pl.BlockSpec((None, nb, H_RET, DK_RET, DV_RET), lambda i: (layer, i, 0, 0, 0)),
        pl.BlockSpec((None, nb, POOL_BUF, D_C), lambda i: (layer, i, 0, 0)),
    ]
    args = [z, z, z, z, z, z, z, tabs["cos"], tabs["sin"],
            lw["ln_g"], lw["ln_b"], lw["wmix_s"], lw["bmap_s"],
            tabs["din"], tabs["dq"], tabs["dk"], tabs["dc"],
            lw["w_pool"], lw["pool_scale"], state_ret, state_pool]
    n_in = len(args)
    aliased = [a_buf, o_buf, c_buf] + ([ret_buf] if ret_buf is not None else [])
    aliases = {n_in + 0: 0, n_in + 1: 1, n_in + 2: 2}
    if ret_buf is not None:
        aliases[n_in + 3] = 4
    in_specs += [_any_spec()] * len(aliased)
    out_specs = [
        pl.BlockSpec((r, D_A), lambda i: (blk0 + i, 0)),
        pl.BlockSpec((r, V_W), lambda i: (blk0 + i, 0)),
        pl.BlockSpec((r, D_C), lambda i: (blk0 + i, 0)),
        pl.BlockSpec((r, D_A), lambda i: (i, 0)),
        pl.BlockSpec((None, nb, H_RET, DK_RET, DV_RET), lambda i: (layer, i, 0, 0, 0)),
        pl.BlockSpec((nb, POOL_BUF, D_C), lambda i: (i, 0, 0)),
    ]
    out_shape = [
        jax.ShapeDtypeStruct((t, D_A), BF16),
        jax.ShapeDtypeStruct((t, V_W), BF16),
        jax.ShapeDtypeStruct((t, D_C), BF16),
        jax.ShapeDtypeStruct((rows, D_A), F32),
        jax.ShapeDtypeStruct((DEPTH, batch, H_RET, DK_RET, DV_RET), F32),
        jax.ShapeDtypeStruct((batch, POOL_BUF, D_C), F32),
    ]
    return pl.pallas_call(
        functools.partial(_mix_sample_body, seq=seq, pos0=pos0, n_alias=len(aliased)),
        grid=(batch // nb,),
        in_specs=in_specs,
        out_specs=out_specs,
        out_shape=out_shape,
        scratch_shapes=[pltpu.VMEM((nb, 2 * SUBLANES + seq, D_C), F32)],
        input_output_aliases=aliases,
        compiler_params=_params(("parallel",), 40),
        name="mix_sample",
    )(*args, *aliased)


def _merge_body(a_ref, o_ref, c_ref, ga0, ga1, gb0, gb1, gc0, gc1, h_ref,
                wa_ref, wb_ref, wc_ref, wo_ref, out_ref):
    def gate(lo, hi):
        return jax.nn.sigmoid(jnp.concatenate([lo[...], hi[...]], axis=-1))

    m = gate(ga0, ga1) * _dot(a_ref[...], wa_ref[...])
    m = m + gate(gb0, gb1) * _dot(o_ref[...], wb_ref[...])
    m = m + gate(gc0, gc1) * _dot(c_ref[...], wc_ref[...])
    out_ref[...] = h_ref[...] + _dot(m.astype(BF16), wo_ref[...])


def _merge(a, o, c, z, h, lw, layer):
    t = h.shape[0]
    tm = 512

    def rows(width, blk=0):
        return pl.BlockSpec((tm, width), lambda i: (i, blk))

    def whole(shape):
        nd = len(shape)
        return pl.BlockSpec((None,) + shape, lambda i: (layer,) + (0,) * nd)

    gates = [rows(HALF, b) for g in (BLK_GA, BLK_GB, BLK_GC) for b in (g, g + 1)]
    return pl.pallas_call(
        _merge_body,
        grid=(t // tm,),
        in_specs=[rows(D_A), rows(V_W), rows(D_C)] + gates + [
            rows(D_MODEL),
            whole((D_A, D_MODEL)), whole((V_W, D_MODEL)), whole((D_C, D_MODEL)),
            whole((D_MODEL, D_MODEL))],
        out_specs=rows(D_MODEL),
        out_shape=jax.ShapeDtypeStruct((t, D_MODEL), F32),
        compiler_params=_params(("parallel",), 48),
        name="merge",
    )(a, o, c, z, z, z, z, z, z, h, lw["w_br_a"], lw["w_br_b"], lw["w_br_c"], lw["w_out"])


def _ffn_body(x_ref, g_ref, wg_ref, wu_ref, wd_ref, out_ref, xn_ref, acc_ref):
    j = pl.program_id(1)

    @pl.when(j == 0)
    def _():
        xn_ref[...] = _rms(x_ref[...], g_ref[...]).astype(BF16)
        acc_ref[...] = jnp.zeros_like(acc_ref)

    xn = xn_ref[...]
    hid = jax.nn.silu(_dot(xn, wg_ref[...])) * _dot(xn, wu_ref[...])
    acc_ref[...] += _dot(hid.astype(BF16), wd_ref[...])

    @pl.when(j == pl.num_programs(1) - 1)
    def _():
        out_ref[...] = x_ref[...] + acc_ref[...]


def _ffn_dense(h, g_all, wg, wu, wd, layer, j):
    t = h.shape[0]
    tm, tf = 512, 1408
    return pl.pallas_call(
        _ffn_body,
        grid=(t // tm, D_FF // tf),
        in_specs=[
            pl.BlockSpec((tm, D_MODEL), lambda i, f: (i, 0)),
            pl.BlockSpec((None, 1, D_MODEL), lambda i, f: (layer, 0, 0)),
            pl.BlockSpec((None, D_MODEL, tf), lambda i, f: (j, 0, f)),
            pl.BlockSpec((None, D_MODEL, tf), lambda i, f: (j, 0, f)),
            pl.BlockSpec((None, tf, D_MODEL), lambda i, f: (j, f, 0)),
        ],
        out_specs=pl.BlockSpec((tm, D_MODEL), lambda i, f: (i, 0)),
        out_shape=jax.ShapeDtypeStruct((t, D_MODEL), F32),
        scratch_shapes=[pltpu.VMEM((tm, D_MODEL), BF16), pltpu.VMEM((tm, D_MODEL), F32)],
        compiler_params=_params(("parallel", "arbitrary"), 48),
        name="ffn_dense",
    )(h, g_all, wg, wu, wd)


def _router_body(x_ref, g_ref, wr_ref, idx_ref, wgt_ref, cnt_ref, carry_scr):
    @pl.when(pl.program_id(0) == 0)
    def _():
        carry_scr[...] = jnp.zeros_like(carry_scr)

    tm = x_ref.shape[0]
    xn = _rms(x_ref[...], g_ref[...])
    logits = jnp.dot(xn, wr_ref[...], preferred_element_type=F32, precision=lax.Precision.HIGHEST)
    lane = lax.broadcasted_iota(I32, logits.shape, 1)
    neg = jnp.float32(-jnp.inf)
    logits = jnp.where(lane < N_EXPERTS, logits, neg)
    v1 = jnp.max(logits, axis=-1, keepdims=True)
    i1 = jnp.min(jnp.where(logits == v1, lane, LANES), axis=-1, keepdims=True)
    rest = jnp.where(lane == i1, neg, logits)
    v2 = jnp.max(rest, axis=-1, keepdims=True)
    i2 = jnp.min(jnp.where(rest == v2, lane, LANES), axis=-1, keepdims=True)
    e2 = jnp.exp(v2 - v1)
    den = 1.0 + e2
    w1 = 1.0 / den
    w2 = e2 / den

    hit1 = lane == i1
    hit2 = lane == i2
    cnt = jnp.where(hit1, 1.0, jnp.where(hit2, 1.0, 0.0))
    r_i = lax.broadcasted_iota(I32, (tm, tm), 0)
    c_i = lax.broadcasted_iota(I32, (tm, tm), 1)
    strictly_lower = jnp.where(c_i < r_i, 1.0, 0.0).astype(BF16)
    before = _dot(strictly_lower, cnt.astype(BF16)) + carry_scr[...]
    rank1 = jnp.sum(jnp.where(hit1, before, 0.0), axis=-1, keepdims=True).astype(I32)
    rank2 = jnp.sum(jnp.where(hit2, before, 0.0), axis=-1, keepdims=True).astype(I32)
    carry_scr[...] += jnp.sum(cnt, axis=0, keepdims=True)

    idx_ref[...] = jnp.where(lane == 0, i1, jnp.where(lane == 1, i2,
                             jnp.where(lane == 2, rank1, jnp.where(lane == 3, rank2, 0))))
    wgt_ref[...] = jnp.where(lane == 0, w1, jnp.where(lane == 1, w2, 0.0))
    cnt_ref[...] = carry_scr[...].astype(I32)


def _router(h, g_all, wr_all, layer, j):
    t = h.shape[0]
    tm = ROW_TILE
    return pl.pallas_call(
        _router_body,
        grid=(t // tm,),
        in_specs=[
            pl.BlockSpec((tm, D_MODEL), lambda i: (i, 0)),
            pl.BlockSpec((None, 1, D_MODEL), lambda i: (layer, 0, 0)),
            pl.BlockSpec((None, D_MODEL, LANES), lambda i: (j, 0, 0)),
        ],
        out_specs=[pl.BlockSpec((tm, LANES), lambda i: (i, 0)),
                   pl.BlockSpec((tm, LANES), lambda i: (i, 0)),
                   pl.BlockSpec((1, LANES), lambda i: (0, 0))],
        out_shape=[jax.ShapeDtypeStruct((t, LANES), I32),
                   jax.ShapeDtypeStruct((t, LANES), F32),
                   jax.ShapeDtypeStruct((1, LANES), I32)],
        scratch_shapes=[pltpu.VMEM((1, LANES), F32)],
        compiler_params=_params(("arbitrary",), 32),
        name="router",
    )(h, g_all, wr_all)


def _routing_plan(idx, cnt, n_tiles):
    e1, e2, r1, r2 = idx[:, 0], idx[:, 1], idx[:, 2], idx[:, 3]
    counts = cnt[0, :N_EXPERTS]
    gsz = ((counts + MOE_TILE - 1) // MOE_TILE) * MOE_TILE
    gend = jnp.cumsum(gsz)
    gstart = gend - gsz
    pos1 = gstart[e1] + r1
    pos2 = gstart[e2] + r2
    n_used = gend[-1] // MOE_TILE
    tile_ids = jnp.arange(n_tiles, dtype=I32)
    texp = jnp.sum((tile_ids * MOE_TILE)[:, None] >= gend[None, :], axis=1).astype(I32)
    texp = jnp.minimum(texp, N_EXPERTS - 1)
    texp = jnp.where(tile_ids < n_used, texp, texp[jnp.maximum(n_used - 1, 0)])
    return pos1.astype(I32), pos2.astype(I32), texp, n_used.astype(I32).reshape(1)


def _row_copy(src_ref, src_row, dst_ref, dst_row, sem):
    return pltpu.make_async_copy(src_ref.at[pl.ds(src_row, 1)], dst_ref.at[pl.ds(dst_row, 1)], sem)


def _dispatch_body(pos1_ref, pos2_ref, h_ref, g_ref, xs_in_ref, xs_ref, hn_scr, sems):
    del xs_in_ref
    tm = h_ref.shape[0]
    base = pl.program_id(0) * tm
    hn_scr[...] = _rms(h_ref[...], g_ref[...])

    def copies(t):
        return (_row_copy(hn_scr, t, xs_ref, pos1_ref[base + t], sems.at[0]),
                _row_copy(hn_scr, t, xs_ref, pos2_ref[base + t], sems.at[1]))

    def start(t, carry):
        for cp in copies(t):
            cp.start()
        return carry

    def wait(t, carry):
        for cp in copies(t):
            cp.wait()
        return carry

    lax.fori_loop(0, tm, start, 0)
    lax.fori_loop(0, tm, wait, 0)


def _dispatch(h, g_all, pos1, pos2, n_rows, layer):
    t = h.shape[0]
    tm = ROW_TILE
    xs0 = jnp.zeros((n_rows, D_MODEL), F32)
    return pl.pallas_call(
        _dispatch_body,
        grid_spec=pltpu.PrefetchScalarGridSpec(
            num_scalar_prefetch=2,
            grid=(t // tm,),
            in_specs=[pl.BlockSpec((tm, D_MODEL), lambda i, p1, p2: (i, 0)),
                      pl.BlockSpec((None, 1, D_MODEL), lambda i, p1, p2: (layer, 0, 0)),
                      _any_spec()],
            out_specs=_any_spec(),
            scratch_shapes=[pltpu.VMEM((tm, D_MODEL), F32), pltpu.SemaphoreType.DMA((2,))],
        ),
        out_shape=jax.ShapeDtypeStruct((n_rows, D_MODEL), F32),
        input_output_aliases={4: 0},
        compiler_params=_params(("arbitrary",), 32),
        name="moe_dispatch",
    )(pos1, pos2, h, g_all, xs0)


def _moe_group_body(texp_ref, nused_ref, x_ref, wg_ref, wu_ref, wd_ref, y_ref, xb_ref, acc_ref):
    i = pl.program_id(0)
    f = pl.program_id(1)

    @pl.when(i < nused_ref[0])
    def _():
        @pl.when(f == 0)
        def _():
            xb_ref[...] = x_ref[...].astype(BF16)
            acc_ref[...] = jnp.zeros_like(acc_ref)

        xb = xb_ref[...]
        hid = jax.nn.silu(_dot(xb, wg_ref[...].astype(BF16))) * _dot(xb, wu_ref[...].astype(BF16))
        acc_ref[...] += _dot(hid.astype(BF16), wd_ref[...].astype(BF16))

        @pl.when(f == pl.num_programs(1) - 1)
        def _():
            y_ref[...] = acc_ref[...]


def _moe_group(xs, texp, n_used, wg, wu, wd, j):
    n_rows = xs.shape[0]
    tm, tf = MOE_TILE, 512
    nf = D_FF_EXP // tf

    def row_blk(i, nu):
        return jnp.minimum(i, nu[0] - 1)

    def f_blk(i, f, nu):
        return jnp.where(i < nu[0], f, nf - 1)

    return pl.pallas_call(
        _moe_group_body,
        grid_spec=pltpu.PrefetchScalarGridSpec(
            num_scalar_prefetch=2,
            grid=(n_rows // tm, nf),
            in_specs=[
                pl.BlockSpec((tm, D_MODEL), lambda i, f, te, nu: (row_blk(i, nu), 0)),
                pl.BlockSpec((None, None, D_MODEL, tf), lambda i, f, te, nu: (j, te[i], 0, f_blk(i, f, nu))),
                pl.BlockSpec((None, None, D_MODEL, tf), lambda i, f, te, nu: (j, te[i], 0, f_blk(i, f, nu))),
                pl.BlockSpec((None, None, tf, D_MODEL), lambda i, f, te, nu: (j, te[i], f_blk(i, f, nu), 0)),
            ],
            out_specs=pl.BlockSpec((tm, D_MODEL), lambda i, f, te, nu: (row_blk(i, nu), 0)),
            scratch_shapes=[pltpu.VMEM((tm, D_MODEL), BF16), pltpu.VMEM((tm, D_MODEL), F32)],
        ),
        out_shape=jax.ShapeDtypeStruct((n_rows, D_MODEL), F32),
        compiler_params=_params(("arbitrary", "arbitrary"), 52),
        name="moe_group",
    )(texp, n_used, xs, wg, wu, wd)


def _combine_body(pos1_ref, pos2_ref, h_ref, wgt_ref, ys_ref, out_ref, y1_scr, y2_scr, sems):
    tm = h_ref.shape[0]
    base = pl.program_id(0) * tm

    def copies(t):
        return (_row_copy(ys_ref, pos1_ref[base + t], y1_scr, t, sems.at[0]),
                _row_copy(ys_ref, pos2_ref[base + t], y2_scr, t, sems.at[1]))

    def start(t, carry):
        for cp in copies(t):
            cp.start()
        return carry

    def wait(t, carry):
        for cp in copies(t):
            cp.wait()
        return carry

    lax.fori_loop(0, tm, start, 0)
    lax.fori_loop(0, tm, wait, 0)
    w = wgt_ref[...]
    out_ref[...] = h_ref[...] + (w[:, 0:1] * y1_scr[...] + w[:, 1:2] * y2_scr[...])


def _combine(h, wgt, ys, pos1, pos2):
    t = h.shape[0]
    tm = ROW_TILE
    return pl.pallas_call(
        _combine_body,
        grid_spec=pltpu.PrefetchScalarGridSpec(
            num_scalar_prefetch=2,
            grid=(t // tm,),
            in_specs=[pl.BlockSpec((tm, D_MODEL), lambda i, p1, p2: (i, 0)),
                      pl.BlockSpec((tm, LANES), lambda i, p1, p2: (i, 0)),
                      _any_spec()],
            out_specs=pl.BlockSpec((tm, D_MODEL), lambda i, p1, p2: (i, 0)),
            scratch_shapes=[pltpu.VMEM((tm, D_MODEL), F32), pltpu.VMEM((tm, D_MODEL), F32),
                            pltpu.SemaphoreType.DMA((2,))],
        ),
        out_shape=jax.ShapeDtypeStruct((t, D_MODEL), F32),
        compiler_params=_params(("arbitrary",), 32),
        name="moe_combine",
    )(pos1, pos2, h, wgt, ys)


def _ffn_moe(h, g_all, wr_all, wg, wu, wd, layer, j):
    t = h.shape[0]
    n_rows = TOP_K * t + N_EXPERTS * MOE_TILE
    idx, wgt, cnt = _router(h, g_all, wr_all, layer, j)
    pos1, pos2, texp, n_used = _routing_plan(idx, cnt, n_rows // MOE_TILE)
    xs = _dispatch(h, g_all, pos1, pos2, n_rows, layer)
    ys = _moe_group(xs, texp, n_used, wg, wu, wd, j)
    return _combine(h, wgt, ys, pos1, pos2)


def _final_norm_body(x_ref, g_ref, o_ref):
    o_ref[...] = _rms(x_ref[...], g_ref[...])


def _final_norm(h, g, row0, rows):
    tm = 1024
    blk0 = row0 // tm
    return pl.pallas_call(
        _final_norm_body,
        grid=(rows // tm,),
        in_specs=[pl.BlockSpec((tm, D_MODEL), lambda i: (blk0 + i, 0)),
                  pl.BlockSpec((1, D_MODEL), lambda i: (0, 0))],
        out_specs=pl.BlockSpec((tm, D_MODEL), lambda i: (i, 0)),
        out_shape=jax.ShapeDtypeStruct((rows, D_MODEL), F32),
        compiler_params=_params(("parallel",), 32),
        name="final_norm",
    )(h, g)


def _rope_tables(pos):
    half = DK_RET // 2
    inv = ROPE_BASE ** (-jnp.arange(half, dtype=F32) / half)
    ang = pos.astype(F32)[:, None] * inv[None, :]
    cos, sin = jnp.cos(ang), jnp.sin(ang)
    return jnp.concatenate([cos, cos], axis=-1), jnp.concatenate([-sin, sin], axis=-1)


def _decay_tables(c):
    log_g = jnp.log1p(-jnp.exp2(-5.0 - jnp.arange(H_RET, dtype=F32)))
    idx = jnp.arange(c, dtype=F32)
    diff = idx[:, None] - idx[None, :]
    causal = diff >= 0
    d_in = jnp.where(causal[None], jnp.exp(log_g[:, None, None] * jnp.where(causal, diff, 0.0)[None]), 0.0)
    d_q = jnp.exp(log_g[:, None] * (idx[None, :] + 1.0))
    d_k = jnp.exp(log_g[:, None] * (c - 1.0 - idx[None, :]))
    d_c = jnp.exp(log_g * c)
    return d_in, d_q, d_k, d_c


def _prompt_tables(seq):
    c = min(CHUNK, seq)
    cos2, sin2 = _rope_tables(jnp.arange(seq, dtype=I32))
    d_in, d_q, d_k, d_c = _decay_tables(c)
    return dict(cos=cos2, sin=sin2, din=d_in, dq=d_q[:, :, None], dk=d_k[:, :, None],
                dc=d_c[:, None, None])


def _sample_tables(seq, pos0):
    nb = SAMPLE_NB
    cos2, sin2 = _rope_tables(pos0 + jnp.arange(seq, dtype=I32))
    d_in, d_q, d_k, d_c = _decay_tables(seq)
    eye = jnp.eye(nb, dtype=F32)
    d_blk = jnp.einsum("ab,hij->haibj", eye, d_in).reshape(H_RET, nb * seq, nb * seq)
    return dict(cos=jnp.tile(cos2, (nb, 1)), sin=jnp.tile(sin2, (nb, 1)), din=d_blk,
                dq=jnp.tile(d_q, (1, nb))[:, :, None], dk=jnp.tile(d_k, (1, nb))[:, :, None],
                dc=d_c[:, None, None])


def kernel(x_prompt, x_sample, state_ret, state_pool, norm1_g, w_in, ln_v_g, ln_v_b, w_spatial, b_spatial, w_pool, pool_scale, w_br_a, w_br_b, w_br_c, w_out, norm2_g, w_ffn_gate, w_ffn_up, w_ffn_down, w_router, w_exp_gate, w_exp_up, w_exp_down, final_norm_g):
    batch, seq, _ = x_prompt.shape
    dec_batch, dec_seq, _ = x_sample.shape
    n_prompt = batch * seq
    n_sample = dec_batch * dec_seq
    assert seq % CHUNK == 0 and dec_seq <= CHUNK and PAST_LEN % CHUNK == 0
    assert n_prompt % (SAMPLE_NB * dec_seq) == 0 and dec_batch % SAMPLE_NB == 0

    lw = dict(
        ln_g=ln_v_g[:, None, :], ln_b=ln_v_b[:, None, :],
        w_spatial=w_spatial,
        bmap_p=jnp.repeat(jnp.swapaxes(b_spatial, 1, 2), DG_A, axis=2),
        wmix_s=jnp.repeat(jnp.transpose(w_spatial[:, :, :dec_seq, :dec_seq], (0, 3, 2, 1)), DG_A, axis=3),
        w_pool=w_pool.astype(BF16), pool_scale=pool_scale[:, None, :],
        w_br_a=w_br_a.astype(BF16), w_br_b=w_br_b.astype(BF16), w_br_c=w_br_c.astype(BF16),
        w_out=w_out.astype(BF16),
    )
    lw["bmap_s"] = lw["bmap_p"][:, :dec_seq, :]
    g1 = norm1_g[:, None, :]
    g2 = norm2_g[:, None, :]
    wfg, wfu, wfd = w_ffn_gate.astype(BF16), w_ffn_up.astype(BF16), w_ffn_down.astype(BF16)
    wr = jnp.pad(w_router, ((0, 0), (0, 0), (0, LANES - N_EXPERTS)))
    tabs_p = _prompt_tables(seq)
    tabs_s = _sample_tables(dec_seq, PAST_LEN)

    h = jnp.concatenate([x_prompt.reshape(n_prompt, D_MODEL),
                         x_sample.reshape(n_sample, D_MODEL)], axis=0)
    ret_p, pool_p, pool_s, v_s = [], [], [], []
    ret_s = None
    for layer in range(DEPTH):
        j = layer // 2
        z = _inproj(h, g1, w_in, layer)
        a, o, c, s_p, b_p = _mix_prompt(z, batch, seq, tabs_p, lw, layer)
        a, o, c, vn, ret_s, b_s = _mix_sample(z, a, o, c, ret_s, n_prompt, dec_batch, dec_seq, PAST_LEN,
                                              tabs_s, lw, state_ret, state_pool, layer)
        h = _merge(a, o, c, z, h, lw, layer)
        if layer % 2 == 0:
            h = _ffn_dense(h, g2, wfg, wfu, wfd, layer, j)
        else:
            h = _ffn_moe(h, g2, wr, w_exp_gate, w_exp_up, w_exp_down, layer, j)
        ret_p.append(s_p)
        pool_p.append(b_p)
        pool_s.append(b_s)
        v_s.append(vn.reshape(dec_batch, dec_seq, D_A))

    g_f = final_norm_g[None, :]
    y_prompt = _final_norm(h, g_f, 0, n_prompt).reshape(batch, seq, D_MODEL)
    y_sample = _final_norm(h, g_f, n_prompt, n_sample).reshape(dec_batch, dec_seq, D_MODEL)
    return (y_prompt, y_sample, jnp.stack(ret_p), ret_s, jnp.stack(pool_p),
            jnp.stack(pool_s), jnp.stack(v_s))
```

```python
import functools

import jax
import jax.numpy as jnp
from jax import lax
from jax.experimental import pallas as pl
from jax.experimental.pallas import tpu as pltpu

F32 = jnp.float32
BF16 = jnp.bfloat16
I32 = jnp.int32

D_MODEL = 1024
DEPTH = 4
PAST_LEN = 16384
D_A = D_MODEL // 2
CHUNK = 128
G_A = 4
DG_A = D_A // G_A
H_RET = 4
DK_RET = D_MODEL // 2 // H_RET
DV_RET = 2 * DK_RET
ROPE_BASE = 10000.0
D_C = D_MODEL // 2
POOL_WINDOWS = (2, 4, 8, 16)
DG_C = D_C // len(POOL_WINDOWS)
POOL_BUF = max(POOL_WINDOWS) - 1
D_FF = 2816
N_EXPERTS = 8
TOP_K = 2
D_FF_EXP = 3584
EPS = 1e-6
QK_W = H_RET * DK_RET
V_W = H_RET * DV_RET
IN_COLS = 2 * D_A + 2 * QK_W + 2 * V_W + D_C + 3 * D_MODEL

LANES = 128
SUBLANES = 8
MIB = 1024 * 1024

BLK_U, BLK_V, BLK_Q, BLK_K, BLK_XC = 0, 1, 2, 3, 8
BLK_VR, BLK_GR = 2, 3
BLK_GA, BLK_GB, BLK_GC = 9, 11, 13
HALF = D_MODEL // 2

MOE_TILE = 1024
ROW_TILE = 512


def _params(sem, vmem_mib):
    return pltpu.CompilerParams(dimension_semantics=sem, vmem_limit_bytes=vmem_mib * MIB)


def _rms(x, g):
    return x * lax.rsqrt(jnp.mean(x * x, axis=-1, keepdims=True) + EPS) * g


def _layer_norm(x, g, b):
    mu = jnp.mean(x, axis=-1, keepdims=True)
    xc = x - mu
    var = jnp.mean(xc * xc, axis=-1, keepdims=True)
    return xc * lax.rsqrt(var + EPS) * g + b


def _dot(a, b):
    return jnp.dot(a, b, preferred_element_type=F32)


def _dot_nt(a, b):
    return lax.dot_general(a, b, (((1,), (1,)), ((), ())), preferred_element_type=F32)


def _dot_tn(a, b):
    return lax.dot_general(a, b, (((0,), (0,)), ((), ())), preferred_element_type=F32)


def _rotary(x, cos2, sin2):
    return x * cos2 + pltpu.roll(x, DK_RET // 2, 1) * sin2


def _any_spec():
    return pl.BlockSpec(memory_space=pl.ANY)


def _norm_cast_body(x_ref, g_ref, o_ref):
    o_ref[...] = _rms(x_ref[...], g_ref[...]).astype(BF16)


def _norm_cast(h, g_all, layer):
    t = h.shape[0]
    tm = 1024
    return pl.pallas_call(
        _norm_cast_body,
        grid=(t // tm,),
        in_specs=[pl.BlockSpec((tm, D_MODEL), lambda i: (i, 0)),
                  pl.BlockSpec((None, 1, D_MODEL), lambda i: (layer, 0, 0))],
        out_specs=pl.BlockSpec((tm, D_MODEL), lambda i: (i, 0)),
        out_shape=jax.ShapeDtypeStruct((t, D_MODEL), BF16),
        compiler_params=_params(("parallel",), 32),
        name="norm_cast",
    )(h, g_all)


def _inproj_body(x_ref, w_ref, o_ref, wb_ref):
    @pl.when(pl.program_id(1) == 0)
    def _():
        wb_ref[...] = w_ref[...].astype(BF16)

    o_ref[...] = _dot(x_ref[...], wb_ref[...])


def _inproj(xn, w_all, layer):
    t = xn.shape[0]
    tm, tn = 1024, 1536
    return pl.pallas_call(
        _inproj_body,
        grid=(IN_COLS // tn, t // tm),
        in_specs=[
            pl.BlockSpec((tm, D_MODEL), lambda j, i: (i, 0)),
            pl.BlockSpec((None, D_MODEL, tn), lambda j, i: (layer, 0, j)),
        ],
        out_specs=pl.BlockSpec((tm, tn), lambda j, i: (i, j)),
        out_shape=jax.ShapeDtypeStruct((t, IN_COLS), F32),
        scratch_shapes=[pltpu.VMEM((D_MODEL, tn), BF16)],
        compiler_params=_params(("parallel", "arbitrary"), 48),
        name="inproj",
    )(xn, w_all)


def _mix_prompt_body(u_ref, v_ref, q_ref, k_ref, vr_ref, gr_ref, xc_ref, cos_ref, sin_ref,
                     lng_ref, lnb_ref, ws_ref, bmap_ref, din_ref, dq_ref, dk_ref, dc_ref,
                     wp_ref, ps_ref,
                     a_ref, o_ref, c_ref, s_out_ref, buf_out_ref,
                     s_scr, xx_scr):
    n = pl.program_id(1)
    c = CHUNK
    hist = 2 * SUBLANES

    @pl.when(n == 0)
    def _():
        s_scr[...] = jnp.zeros_like(s_scr)
        xx_scr[0:hist, :] = jnp.zeros((hist, D_C), F32)

    row = lax.broadcasted_iota(I32, (c, c), 0)
    col = lax.broadcasted_iota(I32, (c, c), 1)

    u = jax.nn.gelu(u_ref[...])
    v = _layer_norm(jax.nn.gelu(v_ref[...]), lng_ref[...], lnb_ref[...])
    for g in range(G_A):
        sl = slice(g * DG_A, (g + 1) * DG_A)
        w = jnp.where(col <= row, ws_ref[g], 0.0).astype(BF16)
        mixed = _dot(w, v[:, sl].astype(BF16)) + bmap_ref[:, sl]
        a_ref[:, sl] = (u[:, sl] * mixed).astype(BF16)

    cos2 = cos_ref[...]
    sin2 = sin_ref[...]
    for h in range(H_RET):
        qs = slice(h * DK_RET, (h + 1) * DK_RET)
        vs = slice(h * DV_RET, (h + 1) * DV_RET)
        qr = _rotary(q_ref[:, qs], cos2, sin2)
        kr = _rotary(k_ref[:, qs], cos2, sin2) * (DK_RET ** -0.5)
        qb = qr.astype(BF16)
        vb = vr_ref[:, vs].astype(BF16)
        scores = _dot_nt(qb, kr.astype(BF16)) * din_ref[h]
        inner = _dot(scores.astype(BF16), vb)
        s_old = s_scr[h]
        cross = _dot(qb, s_old.astype(BF16)) * dq_ref[h]
        kd = (kr * dk_ref[h]).astype(BF16)
        s_scr[h] = dc_ref[h] * s_old + _dot_tn(kd, vb)
        oh = inner + cross
        oh = oh * lax.rsqrt(jnp.mean(oh * oh, axis=-1, keepdims=True) + EPS)
        o_ref[:, vs] = (jax.nn.silu(gr_ref[:, vs]) * oh).astype(BF16)

    xc = xc_ref[...]
    xx_scr[hist:hist + c, :] = xc
    pos1 = n * c + 1 + lax.broadcasted_iota(I32, (c, DG_C), 0)
    for gi, w in enumerate(POOL_WINDOWS):
        sl = slice(gi * DG_C, (gi + 1) * DG_C)
        acc = xx_scr[hist:hist + c, sl]
        for j in range(1, w):
            acc = acc + xx_scr[hist - j:hist - j + c, sl]
        cnt = jnp.minimum(w, pos1).astype(F32)
        pooled = acc / cnt - xc[:, sl]
        c_ref[:, sl] = (_dot(pooled.astype(BF16), wp_ref[gi]) * ps_ref[:, sl]).astype(BF16)
    xx_scr[0:hist, :] = xx_scr[c:c + hist, :]

    @pl.when(n == pl.num_programs(1) - 1)
    def _():
        s_out_ref[...] = s_scr[...]
        buf_out_ref[...] = xx_scr[hist - POOL_BUF:hist, :]


def _mix_prompt(z, batch, seq, tabs, lw, layer):
    c = CHUNK
    nc = seq // c
    t = z.shape[0]

    def zspec(width, blk):
        return pl.BlockSpec((c, width), lambda b, n: (b * nc + n, blk))

    def whole(shape):
        nd = len(shape)
        return pl.BlockSpec((None,) + shape, lambda b, n: (layer,) + (0,) * nd)

    def const(shape):
        nd = len(shape)
        return pl.BlockSpec(shape, lambda b, n: (0,) * nd)

    in_specs = [
        zspec(D_A, BLK_U), zspec(D_A, BLK_V), zspec(QK_W, BLK_Q), zspec(QK_W, BLK_K),
        zspec(V_W, BLK_VR), zspec(V_W, BLK_GR), zspec(D_C, BLK_XC),
        pl.BlockSpec((c, DK_RET), lambda b, n: (n, 0)),
        pl.BlockSpec((c, DK_RET), lambda b, n: (n, 0)),
        whole((1, D_A)), whole((1, D_A)), whole((G_A, c, c)), whole((c, D_A)),
        const((H_RET, c, c)), const((H_RET, c, 1)), const((H_RET, c, 1)), const((H_RET, 1, 1)),
        whole((len(POOL_WINDOWS), DG_C, DG_C)), whole((1, D_C)),
    ]
    out_specs = [
        pl.BlockSpec((c, D_A), lambda b, n: (b * nc + n, 0)),
        pl.BlockSpec((c, V_W), lambda b, n: (b * nc + n, 0)),
        pl.BlockSpec((c, D_C), lambda b, n: (b * nc + n, 0)),
        pl.BlockSpec((None, H_RET, DK_RET, DV_RET), lambda b, n: (b, 0, 0, 0)),
        pl.BlockSpec((None, POOL_BUF, D_C), lambda b, n: (b, 0, 0)),
    ]
    out_shape = [
        jax.ShapeDtypeStruct((t, D_A), BF16),
        jax.ShapeDtypeStruct((t, V_W), BF16),
        jax.ShapeDtypeStruct((t, D_C), BF16),
        jax.ShapeDtypeStruct((batch, H_RET, DK_RET, DV_RET), F32),
        jax.ShapeDtypeStruct((batch, POOL_BUF, D_C), F32),
    ]
    return pl.pallas_call(
        _mix_prompt_body,
        grid=(batch, nc),
        in_specs=in_specs,
        out_specs=out_specs,
        out_shape=out_shape,
        scratch_shapes=[pltpu.VMEM((H_RET, DK_RET, DV_RET), F32),
                        pltpu.VMEM((c + 2 * SUBLANES, D_C), F32)],
        compiler_params=_params(("arbitrary", "arbitrary"), 32),
        name="mix_prompt",
    )(z, z, z, z, z, z, z, tabs["cos"], tabs["sin"],
      lw["ln_g"], lw["ln_b"], lw["w_spatial"], lw["bmap_p"],
      tabs["din"], tabs["dq"], tabs["dk"], tabs["dc"],
      lw["w_pool"], lw["pool_scale"])


SAMPLE_NB = 8


def _mix_sample_body(u_ref, v_ref, q_ref, k_ref, vr_ref, gr_ref, xc_ref, cos_ref, sin_ref,
                     lng_ref, lnb_ref, wmix_ref, bmap_ref, din_ref, dq_ref, dk_ref, dc_ref,
                     wp_ref, ps_ref, s_in_ref, buf_in_ref, *rest, seq, pos0, n_alias):
    a_ref, o_ref, c_ref, vn_ref, s_out_ref, buf_out_ref, xx_scr = rest[n_alias:]
    nb = SAMPLE_NB
    r = nb * seq
    hist = 2 * SUBLANES

    u = jax.nn.gelu(u_ref[...])
    v = _layer_norm(jax.nn.gelu(v_ref[...]), lng_ref[...], lnb_ref[...])
    vn_ref[...] = v
    v3 = v.reshape(nb, seq, D_A)
    t_idx = lax.broadcasted_iota(I32, (seq, D_A), 0)
    mixed = jnp.broadcast_to(bmap_ref[...][None], (nb, seq, D_A))
    for s in range(seq):
        w_s = jnp.where(t_idx >= s, wmix_ref[s], 0.0)
        mixed = mixed + w_s[None] * v3[:, s:s + 1, :]
    a_ref[...] = (u * mixed.reshape(r, D_A)).astype(BF16)

    cos2 = cos_ref[...]
    sin2 = sin_ref[...]
    rowb = lax.broadcasted_iota(I32, (r, DV_RET), 0) // seq
    for h in range(H_RET):
        qs = slice(h * DK_RET, (h + 1) * DK_RET)
        vs = slice(h * DV_RET, (h + 1) * DV_RET)
        qr = _rotary(q_ref[:, qs], cos2, sin2)
        kr = _rotary(k_ref[:, qs], cos2, sin2) * (DK_RET ** -0.5)
        qb = qr.astype(BF16)
        vh = vr_ref[:, vs]
        scores = _dot_nt(qb, kr.astype(BF16)) * din_ref[h]
        inner = _dot(scores.astype(BF16), vh.astype(BF16))
        kd = (kr * dk_ref[h]).astype(BF16)
        cross = jnp.zeros((r, DV_RET), F32)
        for b in range(nb):
            s_old = s_in_ref[b, h]
            cross = jnp.where(rowb == b, _dot(qb, s_old.astype(BF16)), cross)
            vm = jnp.where(rowb == b, vh, 0.0).astype(BF16)
            s_out_ref[b, h] = dc_ref[h] * s_old + _dot_tn(kd, vm)
        oh = inner + cross * dq_ref[h]
        oh = oh * lax.rsqrt(jnp.mean(oh * oh, axis=-1, keepdims=True) + EPS)
        o_ref[:, vs] = (jax.nn.silu(gr_ref[:, vs]) * oh).astype(BF16)

    xc = xc_ref[...]
    xx_scr[:, hist - POOL_BUF:hist, :] = buf_in_ref[...]
    xx_scr[:, hist:hist + seq, :] = xc.reshape(nb, seq, D_C)
    pos1 = pos0 + 1 + lax.broadcasted_iota(I32, (seq, DG_C), 0)
    for gi, w in enumerate(POOL_WINDOWS):
        sl = slice(gi * DG_C, (gi + 1) * DG_C)
        acc = xx_scr[:, hist:hist + seq, sl]
        for j in range(1, w):
            acc = acc + xx_scr[:, hist - j:hist - j + seq, sl]
        cnt = jnp.minimum(w, pos1).astype(F32)
        pooled = (acc / cnt[None]).reshape(r, DG_C) - xc[:, sl]
        c_ref[:, sl] = (_dot(pooled.astype(BF16), wp_ref[gi]) * ps_ref[:, sl]).astype(BF16)
    buf_out_ref[...] = xx_scr[:, hist + seq - POOL_BUF:hist + seq, :]


def _mix_sample(z, a_buf, o_buf, c_buf, ret_buf, row0, batch, seq, pos0, tabs, lw,
                state_ret, state_pool, layer):
    nb = SAMPLE_NB
    r = nb * seq
    blk0 = row0 // r
    rows = batch * seq
    t = z.shape[0]

    def zspec(width, blk):
        return pl.BlockSpec((r, width), lambda i: (blk0 + i, blk))

    def whole(shape):
        nd = len(shape)
        return pl.BlockSpec((None,) + shape, lambda i: (layer,) + (0,) * nd)

    def const(shape):
        nd = len(shape)
        return pl.BlockSpec(shape, lambda i: (0,) * nd)

    in_specs = [
        zspec(D_A, BLK_U), zspec(D_A, BLK_V), zspec(QK_W, BLK_Q), zspec(QK_W, BLK_K),
        zspec(V_W, BLK_VR), zspec(V_W, BLK_GR), zspec(D_C, BLK_XC),
        const((r, DK_RET)), const((r, DK_RET)),
        whole((1, D_A)), whole((1, D_A)), whole((seq, seq, D_A)), whole((seq, D_A)),
        const((H_RET, r, r)), const((H_RET, r, 1)), const((H_RET, r, 1)), const((H_RET, 1, 1)),
        whole((len(POOL_WINDOWS), DG_C, DG_C)), whole((1, D_C)),
        pl.BlockSpec((None, nb, H_RET, DK_RET, DV_RET), lambda i: (layer, i, 0, 0, 0)),
        pl.BlockSpec((None, nb, POOL_BUF, D_C), lambda i: (layer, i, 0, 0)),
    ]
    args = [z, z, z, z, z, z, z, tabs["cos"], tabs["sin"],
            lw["ln_g"], lw["ln_b"], lw["wmix_s"], lw["bmap_s"],
            tabs["din"], tabs["dq"], tabs["dk"], tabs["dc"],
            lw["w_pool"], lw["pool_scale"], state_ret, state_pool]
    n_in = len(args)
    aliased = [a_buf, o_buf, c_buf] + ([ret_buf] if ret_buf is not None else [])
    aliases = {n_in + 0: 0, n_in + 1: 1, n_in + 2: 2}
    if ret_buf is not None:
        aliases[n_in + 3] = 4
    in_specs += [_any_spec()] * len(aliased)
    out_specs = [
        pl.BlockSpec((r, D_A), lambda i: (blk0 + i, 0)),
        pl.BlockSpec((r, V_W), lambda i: (blk0 + i, 0)),
        pl.BlockSpec((r, D_C), lambda i: (blk0 + i, 0)),
        pl.BlockSpec((r, D_A), lambda i: (i, 0)),
        pl.BlockSpec((None, nb, H_RET, DK_RET, DV_RET), lambda i: (layer, i, 0, 0, 0)),
        pl.BlockSpec((nb, POOL_BUF, D_C), lambda i: (i, 0, 0)),
    ]
    out_shape = [
        jax.ShapeDtypeStruct((t, D_A), BF16),
        jax.ShapeDtypeStruct((t, V_W), BF16),
        jax.ShapeDtypeStruct((t, D_C), BF16),
        jax.ShapeDtypeStruct((rows, D_A), F32),
        jax.ShapeDtypeStruct((DEPTH, batch, H_RET, DK_RET, DV_RET), F32),
        jax.ShapeDtypeStruct((batch, POOL_BUF, D_C), F32),
    ]
    return pl.pallas_call(
        functools.partial(_mix_sample_body, seq=seq, pos0=pos0, n_alias=len(aliased)),
        grid=(batch // nb,),
        in_specs=in_specs,
        out_specs=out_specs,
        out_shape=out_shape,
        scratch_shapes=[pltpu.VMEM((nb, 2 * SUBLANES + seq, D_C), F32)],
        input_output_aliases=aliases,
        compiler_params=_params(("parallel",), 40),
        name="mix_sample",
    )(*args, *aliased)


def _merge_body(a_ref, o_ref, c_ref, ga0, ga1, gb0, gb1, gc0, gc1, h_ref,
                wa_ref, wb_ref, wc_ref, wo_ref, out_ref):
    def gate(lo, hi):
        return jax.nn.sigmoid(jnp.concatenate([lo[...], hi[...]], axis=-1))

    m = gate(ga0, ga1) * _dot(a_ref[...], wa_ref[...])
    m = m + gate(gb0, gb1) * _dot(o_ref[...], wb_ref[...])
    m = m + gate(gc0, gc1) * _dot(c_ref[...], wc_ref[...])
    out_ref[...] = h_ref[...] + _dot(m.astype(BF16), wo_ref[...])


def _merge(a, o, c, z, h, lw, layer):
    t = h.shape[0]
    tm = 512

    def rows(width, blk=0):
        return pl.BlockSpec((tm, width), lambda i: (i, blk))

    def whole(shape):
        nd = len(shape)
        return pl.BlockSpec((None,) + shape, lambda i: (layer,) + (0,) * nd)

    gates = [rows(HALF, b) for g in (BLK_GA, BLK_GB, BLK_GC) for b in (g, g + 1)]
    return pl.pallas_call(
        _merge_body,
        grid=(t // tm,),
        in_specs=[rows(D_A), rows(V_W), rows(D_C)] + gates + [
            rows(D_MODEL),
            whole((D_A, D_MODEL)), whole((V_W, D_MODEL)), whole((D_C, D_MODEL)),
            whole((D_MODEL, D_MODEL))],
        out_specs=rows(D_MODEL),
        out_shape=jax.ShapeDtypeStruct((t, D_MODEL), F32),
        compiler_params=_params(("parallel",), 48),
        name="merge",
    )(a, o, c, z, z, z, z, z, z, h, lw["w_br_a"], lw["w_br_b"], lw["w_br_c"], lw["w_out"])


def _ffn_body(x_ref, g_ref, gn_ref, wg_ref, wu_ref, wd_ref, out_ref, xnext_ref, xn_ref, acc_ref):
    j = pl.program_id(1)

    @pl.when(j == 0)
    def _():
        xn_ref[...] = _rms(x_ref[...], g_ref[...]).astype(BF16)
        acc_ref[...] = jnp.zeros_like(acc_ref)

    xn = xn_ref[...]
    hid = jax.nn.silu(_dot(xn, wg_ref[...])) * _dot(xn, wu_ref[...])
    acc_ref[...] += _dot(hid.astype(BF16), wd_ref[...])

    @pl.when(j == pl.num_programs(1) - 1)
    def _():
        out = x_ref[...] + acc_ref[...]
        out_ref[...] = out
        xnext_ref[...] = _rms(out, gn_ref[...]).astype(BF16)


def _ffn_dense(h, g_all, g_next_all, wg, wu, wd, layer, j):
    t = h.shape[0]
    tm, tf = 512, 1408
    return pl.pallas_call(
        _ffn_body,
        grid=(t // tm, D_FF // tf),
        in_specs=[
            pl.BlockSpec((tm, D_MODEL), lambda i, f: (i, 0)),
            pl.BlockSpec((None, 1, D_MODEL), lambda i, f: (layer, 0, 0)),
            pl.BlockSpec((None, 1, D_MODEL), lambda i, f: (layer + 1, 0, 0)),
            pl.BlockSpec((None, D_MODEL, tf), lambda i, f: (j, 0, f)),
            pl.BlockSpec((None, D_MODEL, tf), lambda i, f: (j, 0, f)),
            pl.BlockSpec((None, tf, D_MODEL), lambda i, f: (j, f, 0)),
        ],
        out_specs=[pl.BlockSpec((tm, D_MODEL), lambda i, f: (i, 0)),
                   pl.BlockSpec((tm, D_MODEL), lambda i, f: (i, 0))],
        out_shape=[jax.ShapeDtypeStruct((t, D_MODEL), F32),
                   jax.ShapeDtypeStruct((t, D_MODEL), BF16)],
        scratch_shapes=[pltpu.VMEM((tm, D_MODEL), BF16), pltpu.VMEM((tm, D_MODEL), F32)],
        compiler_params=_params(("parallel", "arbitrary"), 48),
        name="ffn_dense",
    )(h, g_all, g_next_all, wg, wu, wd)


def _router_body(x_ref, g_ref, wr_ref, idx_ref, wgt_ref, cnt_ref, carry_scr):
    @pl.when(pl.program_id(0) == 0)
    def _():
        carry_scr[...] = jnp.zeros_like(carry_scr)

    tm = x_ref.shape[0]
    xn = _rms(x_ref[...], g_ref[...])
    logits = jnp.dot(xn, wr_ref[...], preferred_element_type=F32, precision=lax.Precision.HIGHEST)
    lane = lax.broadcasted_iota(I32, logits.shape, 1)
    neg = jnp.float32(-jnp.inf)
    logits = jnp.where(lane < N_EXPERTS, logits, neg)
    v1 = jnp.max(logits, axis=-1, keepdims=True)
    i1 = jnp.min(jnp.where(logits == v1, lane, LANES), axis=-1, keepdims=True)
    rest = jnp.where(lane == i1, neg, logits)
    v2 = jnp.max(rest, axis=-1, keepdims=True)
    i2 = jnp.min(jnp.where(rest == v2, lane, LANES), axis=-1, keepdims=True)
    e2 = jnp.exp(v2 - v1)
    den = 1.0 + e2
    w1 = 1.0 / den
    w2 = e2 / den

    hit1 = lane == i1
    hit2 = lane == i2
    cnt = jnp.where(hit1, 1.0, jnp.where(hit2, 1.0, 0.0))
    r_i = lax.broadcasted_iota(I32, (tm, tm), 0)
    c_i = lax.broadcasted_iota(I32, (tm, tm), 1)
    strictly_lower = jnp.where(c_i < r_i, 1.0, 0.0).astype(BF16)
    before = _dot(strictly_lower, cnt.astype(BF16)) + carry_scr[...]
    rank1 = jnp.sum(jnp.where(hit1, before, 0.0), axis=-1, keepdims=True).astype(I32)
    rank2 = jnp.sum(jnp.where(hit2, before, 0.0), axis=-1, keepdims=True).astype(I32)
    carry_scr[...] += jnp.sum(cnt, axis=0, keepdims=True)

    idx_ref[...] = jnp.where(lane == 0, i1, jnp.where(lane == 1, i2,
                             jnp.where(lane == 2, rank1, jnp.where(lane == 3, rank2, 0))))
    wgt_ref[...] = jnp.where(lane == 0, w1, jnp.where(lane == 1, w2, 0.0))
    cnt_ref[...] = carry_scr[...].astype(I32)


def _router(h, g_all, wr_all, layer, j):
    t = h.shape[0]
    tm = ROW_TILE
    return pl.pallas_call(
        _router_body,
        grid=(t // tm,),
        in_specs=[
            pl.BlockSpec((tm, D_MODEL), lambda i: (i, 0)),
            pl.BlockSpec((None, 1, D_MODEL), lambda i: (layer, 0, 0)),
            pl.BlockSpec((None, D_MODEL, LANES), lambda i: (j, 0, 0)),
        ],
        out_specs=[pl.BlockSpec((tm, LANES), lambda i: (i, 0)),
                   pl.BlockSpec((tm, LANES), lambda i: (i, 0)),
                   pl.BlockSpec((1, LANES), lambda i: (0, 0))],
        out_shape=[jax.ShapeDtypeStruct((t, LANES), I32),
                   jax.ShapeDtypeStruct((t, LANES), F32),
                   jax.ShapeDtypeStruct((1, LANES), I32)],
        scratch_shapes=[pltpu.VMEM((1, LANES), F32)],
        compiler_params=_params(("arbitrary",), 32),
        name="router",
    )(h, g_all, wr_all)


def _routing_plan(idx, cnt, n_tiles):
    e1, e2, r1, r2 = idx[:, 0], idx[:, 1], idx[:, 2], idx[:, 3]
    counts = cnt[0, :N_EXPERTS]
    gsz = ((counts + MOE_TILE - 1) // MOE_TILE) * MOE_TILE
    gend = jnp.cumsum(gsz)
    gstart = gend - gsz
    pos1 = gstart[e1] + r1
    pos2 = gstart[e2] + r2
    n_used = gend[-1] // MOE_TILE
    tile_ids = jnp.arange(n_tiles, dtype=I32)
    texp = jnp.sum((tile_ids * MOE_TILE)[:, None] >= gend[None, :], axis=1).astype(I32)
    texp = jnp.minimum(texp, N_EXPERTS - 1)
    texp = jnp.where(tile_ids < n_used, texp, texp[jnp.maximum(n_used - 1, 0)])
    return pos1.astype(I32), pos2.astype(I32), texp, n_used.astype(I32).reshape(1), gend.astype(I32)


TOKEN_ROWS = D_MODEL // LANES
assert TOKEN_ROWS == SUBLANES


def _token_tile(ref, token):
    return ref.at[pl.ds(pl.multiple_of(token * TOKEN_ROWS, TOKEN_ROWS), TOKEN_ROWS)]


def _store_token_major(dst_ref, x):
    n = x.shape[0]
    for c in range(TOKEN_ROWS):
        dst_ref[pl.ds(c, n, stride=TOKEN_ROWS), :] = x[:, c * LANES:(c + 1) * LANES]


def _load_token_major(src_ref, n):
    return jnp.concatenate([src_ref[pl.ds(c, n, stride=TOKEN_ROWS), :] for c in range(TOKEN_ROWS)],
                           axis=1)


def _dispatch_body(pos1_ref, pos2_ref, gend_ref, h_ref, g_ref, xs_ref, hn_scr, sems):
    tm = h_ref.shape[0]
    i = pl.program_id(0)
    base = i * tm
    step_rows = tm * TOKEN_ROWS

    @pl.when(i == 0)
    def _():
        hn_scr[...] = jnp.zeros_like(hn_scr)
        for e in range(N_EXPERTS):
            end = gend_ref[e]
            start = gend_ref[e - 1] if e else 0

            @pl.when(end > start)
            def _():
                for part in range(MOE_TILE // tm):
                    row0 = pl.multiple_of((end - MOE_TILE + part * tm) * TOKEN_ROWS, TOKEN_ROWS)
                    cp = pltpu.make_async_copy(hn_scr, xs_ref.at[pl.ds(row0, step_rows)], sems.at[0])
                    cp.start()
                    cp.wait()

    _store_token_major(hn_scr, _rms(h_ref[...], g_ref[...]))

    def start(t, carry):
        src = _token_tile(hn_scr, t)
        pltpu.make_async_copy(src, _token_tile(xs_ref, pos1_ref[base + t]), sems.at[0]).start(priority=0)
        pltpu.make_async_copy(src, _token_tile(xs_ref, pos2_ref[base + t]), sems.at[1]).start(priority=1)
        return carry

    lax.fori_loop(0, tm, start, 0, unroll=8)
    for s in range(TOP_K):
        pltpu.make_async_copy(hn_scr, xs_ref.at[pl.ds(0, step_rows)], sems.at[s]).wait()


def _dispatch(h, g_all, pos1, pos2, gend, n_rows, layer):
    t = h.shape[0]
    tm = ROW_TILE
    return pl.pallas_call(
        _dispatch_body,
        grid_spec=pltpu.PrefetchScalarGridSpec(
            num_scalar_prefetch=3,
            grid=(t // tm,),
            in_specs=[pl.BlockSpec((tm, D_MODEL), lambda i, *_: (i, 0)),
                      pl.BlockSpec((None, 1, D_MODEL), lambda i, *_: (layer, 0, 0))],
            out_specs=_any_spec(),
            scratch_shapes=[pltpu.VMEM((tm * TOKEN_ROWS, LANES), F32), pltpu.SemaphoreType.DMA((TOP_K,))],
        ),
        out_shape=jax.ShapeDtypeStruct((n_rows * TOKEN_ROWS, LANES), F32),
        compiler_params=_params(("arbitrary",), 32),
        name="moe_dispatch",
    )(pos1, pos2, gend, h, g_all)


def _moe_group_body(texp_ref, nused_ref, x_ref, wg_ref, wu_ref, wd_ref, y_ref, xb_ref, acc_ref):
    i = pl.program_id(0)
    f = pl.program_id(1)
    tm = xb_ref.shape[0]

    @pl.when(i < nused_ref[0])
    def _():
        @pl.when(f == 0)
        def _():
            xb_ref[...] = _load_token_major(x_ref, tm).astype(BF16)
            acc_ref[...] = jnp.zeros_like(acc_ref)

        xb = xb_ref[...]
        hid = jax.nn.silu(_dot(xb, wg_ref[...].astype(BF16))) * _dot(xb, wu_ref[...].astype(BF16))
        acc_ref[...] += _dot(hid.astype(BF16), wd_ref[...].astype(BF16))

        @pl.when(f == pl.num_programs(1) - 1)
        def _():
            _store_token_major(y_ref, acc_ref[...])


def _moe_group(xs, texp, n_used, wg, wu, wd, j):
    n_rows = xs.shape[0] // TOKEN_ROWS
    tm, tf = MOE_TILE, 512
    nf = D_FF_EXP // tf

    def row_blk(i, nu):
        return jnp.maximum(jnp.minimum(i, nu[0] - 1), 0)

    def f_blk(i, f, nu):
        return jnp.where(i < nu[0], f, nf - 1)

    return pl.pallas_call(
        _moe_group_body,
        grid_spec=pltpu.PrefetchScalarGridSpec(
            num_scalar_prefetch=2,
            grid=(n_rows // tm, nf),
            in_specs=[
                pl.BlockSpec((tm * TOKEN_ROWS, LANES), lambda i, f, te, nu: (row_blk(i, nu), 0)),
                pl.BlockSpec((None, None, D_MODEL, tf), lambda i, f, te, nu: (j, te[i], 0, f_blk(i, f, nu))),
                pl.BlockSpec((None, None, D_MODEL, tf), lambda i, f, te, nu: (j, te[i], 0, f_blk(i, f, nu))),
                pl.BlockSpec((None, None, tf, D_MODEL), lambda i, f, te, nu: (j, te[i], f_blk(i, f, nu), 0)),
            ],
            out_specs=pl.BlockSpec((tm * TOKEN_ROWS, LANES), lambda i, f, te, nu: (row_blk(i, nu), 0)),
            scratch_shapes=[pltpu.VMEM((tm, D_MODEL), BF16), pltpu.VMEM((tm, D_MODEL), F32)],
        ),
        out_shape=jax.ShapeDtypeStruct((n_rows * TOKEN_ROWS, LANES), F32),
        compiler_params=_params(("arbitrary", "arbitrary"), 52),
        name="moe_group",
    )(texp, n_used, xs, wg, wu, wd)


def _combine_body(pos1_ref, pos2_ref, h_ref, wgt_ref, gn_ref, ys_ref, *rest, last):
    *out_refs, y1_scr, y2_scr, sems = rest
    tm = h_ref.shape[0]
    base = pl.program_id(0) * tm

    def start(t, carry):
        pltpu.make_async_copy(_token_tile(ys_ref, pos1_ref[base + t]), _token_tile(y1_scr, t),
                              sems.at[0]).start(priority=0)
        pltpu.make_async_copy(_token_tile(ys_ref, pos2_ref[base + t]), _token_tile(y2_scr, t),
                              sems.at[1]).start(priority=1)
        return carry

    lax.fori_loop(0, tm, start, 0, unroll=8)
    for s, scr in enumerate((y1_scr, y2_scr)):
        pltpu.make_async_copy(ys_ref.at[pl.ds(0, tm * TOKEN_ROWS)], scr, sems.at[s]).wait()
    w = wgt_ref[...]
    out = h_ref[...] + (w[:, 0:1] * _load_token_major(y1_scr, tm)
                        + w[:, 1:2] * _load_token_major(y2_scr, tm))
    if last:
        out_refs[0][...] = _rms(out, gn_ref[...])
    else:
        out_refs[0][...] = out
        out_refs[1][...] = _rms(out, gn_ref[...]).astype(BF16)


def _combine(h, wgt, g_next, ys, pos1, pos2, last):
    t = h.shape[0]
    tm = ROW_TILE
    rows = pl.BlockSpec((tm, D_MODEL), lambda i, *_: (i, 0))
    full = jax.ShapeDtypeStruct((t, D_MODEL), F32)
    return pl.pallas_call(
        functools.partial(_combine_body, last=last),
        grid_spec=pltpu.PrefetchScalarGridSpec(
            num_scalar_prefetch=2,
            grid=(t // tm,),
            in_specs=[rows,
                      pl.BlockSpec((tm, LANES), lambda i, *_: (i, 0)),
                      pl.BlockSpec((1, D_MODEL), lambda i, *_: (0, 0)),
                      _any_spec()],
            out_specs=[rows] if last else [rows, rows],
            scratch_shapes=[pltpu.VMEM((tm * TOKEN_ROWS, LANES), F32),
                            pltpu.VMEM((tm * TOKEN_ROWS, LANES), F32),
                            pltpu.SemaphoreType.DMA((TOP_K,))],
        ),
        out_shape=[full] if last else [full, jax.ShapeDtypeStruct((t, D_MODEL), BF16)],
        compiler_params=_params(("arbitrary",), 32),
        name="moe_combine",
    )(pos1, pos2, h, wgt, g_next, ys)


def _ffn_moe(h, g_all, g_next, wr_all, wg, wu, wd, layer, j, last):
    t = h.shape[0]
    n_rows = TOP_K * t + N_EXPERTS * MOE_TILE
    idx, wgt, cnt = _router(h, g_all, wr_all, layer, j)
    pos1, pos2, texp, n_used, gend = _routing_plan(idx, cnt, n_rows // MOE_TILE)
    xs = _dispatch(h, g_all, pos1, pos2, gend, n_rows, layer)
    ys = _moe_group(xs, texp, n_used, wg, wu, wd, j)
    return _combine(h, wgt, g_next, ys, pos1, pos2, last)


def _rope_tables(pos):
    half = DK_RET // 2
    inv = ROPE_BASE ** (-jnp.arange(half, dtype=F32) / half)
    ang = pos.astype(F32)[:, None] * inv[None, :]
    cos, sin = jnp.cos(ang), jnp.sin(ang)
    return jnp.concatenate([cos, cos], axis=-1), jnp.concatenate([-sin, sin], axis=-1)


def _decay_tables(c):
    log_g = jnp.log1p(-jnp.exp2(-5.0 - jnp.arange(H_RET, dtype=F32)))
    idx = jnp.arange(c, dtype=F32)
    diff = idx[:, None] - idx[None, :]
    causal = diff >= 0
    d_in = jnp.where(causal[None], jnp.exp(log_g[:, None, None] * jnp.where(causal, diff, 0.0)[None]), 0.0)
    d_q = jnp.exp(log_g[:, None] * (idx[None, :] + 1.0))
    d_k = jnp.exp(log_g[:, None] * (c - 1.0 - idx[None, :]))
    d_c = jnp.exp(log_g * c)
    return d_in, d_q, d_k, d_c


def _prompt_tables(seq):
    c = min(CHUNK, seq)
    cos2, sin2 = _rope_tables(jnp.arange(seq, dtype=I32))
    d_in, d_q, d_k, d_c = _decay_tables(c)
    return dict(cos=cos2, sin=sin2, din=d_in, dq=d_q[:, :, None], dk=d_k[:, :, None],
                dc=d_c[:, None, None])


def _sample_tables(seq, pos0):
    nb = SAMPLE_NB
    cos2, sin2 = _rope_tables(pos0 + jnp.arange(seq, dtype=I32))
    d_in, d_q, d_k, d_c = _decay_tables(seq)
    eye = jnp.eye(nb, dtype=F32)
    d_blk = jnp.einsum("ab,hij->haibj", eye, d_in).reshape(H_RET, nb * seq, nb * seq)
    return dict(cos=jnp.tile(cos2, (nb, 1)), sin=jnp.tile(sin2, (nb, 1)), din=d_blk,
                dq=jnp.tile(d_q, (1, nb))[:, :, None], dk=jnp.tile(d_k, (1, nb))[:, :, None],
                dc=d_c[:, None, None])


def kernel(x_prompt, x_sample, state_ret, state_pool, norm1_g, w_in, ln_v_g, ln_v_b, w_spatial, b_spatial, w_pool, pool_scale, w_br_a, w_br_b, w_br_c, w_out, norm2_g, w_ffn_gate, w_ffn_up, w_ffn_down, w_router, w_exp_gate, w_exp_up, w_exp_down, final_norm_g):
    batch, seq, _ = x_prompt.shape
    dec_batch, dec_seq, _ = x_sample.shape
    n_prompt = batch * seq
    n_sample = dec_batch * dec_seq
    assert seq % CHUNK == 0 and dec_seq <= CHUNK and PAST_LEN % CHUNK == 0
    assert n_prompt % (SAMPLE_NB * dec_seq) == 0 and dec_batch % SAMPLE_NB == 0

    lw = dict(
        ln_g=ln_v_g[:, None, :], ln_b=ln_v_b[:, None, :],
        w_spatial=w_spatial,
        bmap_p=jnp.repeat(jnp.swapaxes(b_spatial, 1, 2), DG_A, axis=2),
        wmix_s=jnp.repeat(jnp.transpose(w_spatial[:, :, :dec_seq, :dec_seq], (0, 3, 2, 1)), DG_A, axis=3),
        w_pool=w_pool.astype(BF16), pool_scale=pool_scale[:, None, :],
        w_br_a=w_br_a.astype(BF16), w_br_b=w_br_b.astype(BF16), w_br_c=w_br_c.astype(BF16),
        w_out=w_out.astype(BF16),
    )
    lw["bmap_s"] = lw["bmap_p"][:, :dec_seq, :]
    g1 = norm1_g[:, None, :]
    g2 = norm2_g[:, None, :]
    wfg, wfu, wfd = w_ffn_gate.astype(BF16), w_ffn_up.astype(BF16), w_ffn_down.astype(BF16)
    wr = jnp.pad(w_router, ((0, 0), (0, 0), (0, LANES - N_EXPERTS)))
    tabs_p = _prompt_tables(seq)
    tabs_s = _sample_tables(dec_seq, PAST_LEN)

    h = jnp.concatenate([x_prompt.reshape(n_prompt, D_MODEL),
                         x_sample.reshape(n_sample, D_MODEL)], axis=0)
    ret_p, pool_p, pool_s, v_s = [], [], [], []
    ret_s = None
    assert DEPTH % 2 == 0
    xn = _norm_cast(h, g1, 0)
    for layer in range(DEPTH):
        j = layer // 2
        last = layer == DEPTH - 1
        z = _inproj(xn, w_in, layer)
        a, o, c, s_p, b_p = _mix_prompt(z, batch, seq, tabs_p, lw, layer)
        a, o, c, vn, ret_s, b_s = _mix_sample(z, a, o, c, ret_s, n_prompt, dec_batch, dec_seq, PAST_LEN,
                                              tabs_s, lw, state_ret, state_pool, layer)
        h = _merge(a, o, c, z, h, lw, layer)
        if layer % 2 == 0:
            h, xn = _ffn_dense(h, g2, g1, wfg, wfu, wfd, layer, j)
        else:
            g_next = final_norm_g[None, :] if last else g1[layer + 1]
            res = _ffn_moe(h, g2, g_next, wr, w_exp_gate, w_exp_up, w_exp_down, layer, j, last)
            if last:
                y, = res
            else:
                h, xn = res
        ret_p.append(s_p)
        pool_p.append(b_p)
        pool_s.append(b_s)
        v_s.append(vn.reshape(dec_batch, dec_seq, D_A))

    y_prompt = y[:n_prompt].reshape(batch, seq, D_MODEL)
    y_sample = y[n_prompt:].reshape(dec_batch, dec_seq, D_MODEL)
    return (y_prompt, y_sample, jnp.stack(ret_p), ret_s, jnp.stack(pool_p),
            jnp.stack(pool_s), jnp.stack(v_s))
```

```python
import functools

import jax
import jax.numpy as jnp
from jax import lax
from jax.experimental import pallas as pl
from jax.experimental.pallas import tpu as pltpu

F32 = jnp.float32
BF16 = jnp.bfloat16
I32 = jnp.int32

D_MODEL = 1024
DEPTH = 4
PAST_LEN = 16384
D_A = D_MODEL // 2
CHUNK = 128
G_A = 4
DG_A = D_A // G_A
H_RET = 4
DK_RET = D_MODEL // 2 // H_RET
DV_RET = 2 * DK_RET
ROPE_BASE = 10000.0
D_C = D_MODEL // 2
POOL_WINDOWS = (2, 4, 8, 16)
DG_C = D_C // len(POOL_WINDOWS)
POOL_BUF = max(POOL_WINDOWS) - 1
D_FF = 2816
N_EXPERTS = 8
TOP_K = 2
D_FF_EXP = 3584
EPS = 1e-6
QK_W = H_RET * DK_RET
V_W = H_RET * DV_RET
IN_COLS = 2 * D_A + 2 * QK_W + 2 * V_W + D_C + 3 * D_MODEL

LANES = 128
SUBLANES = 8
MIB = 1024 * 1024

BLK_U, BLK_V, BLK_Q, BLK_K, BLK_XC = 0, 1, 2, 3, 8
BLK_VR, BLK_GR = 2, 3
MIX_COLS = 2 * D_A + 2 * QK_W + 2 * V_W + D_C

MOE_TILE = 1024
ROW_TILE = 512


def _params(sem, vmem_mib):
    return pltpu.CompilerParams(dimension_semantics=sem, vmem_limit_bytes=vmem_mib * MIB)


def _rms(x, g):
    return x * lax.rsqrt(jnp.mean(x * x, axis=-1, keepdims=True) + EPS) * g


def _layer_norm(x, g, b):
    mu = jnp.mean(x, axis=-1, keepdims=True)
    xc = x - mu
    var = jnp.mean(xc * xc, axis=-1, keepdims=True)
    return xc * lax.rsqrt(var + EPS) * g + b


def _dot(a, b):
    return jnp.dot(a, b, preferred_element_type=F32)


def _dot_nt(a, b):
    return lax.dot_general(a, b, (((1,), (1,)), ((), ())), preferred_element_type=F32)


def _dot_tn(a, b):
    return lax.dot_general(a, b, (((0,), (0,)), ((), ())), preferred_element_type=F32)


def _rotary(x, cos2, sin2):
    return x * cos2 + pltpu.roll(x, DK_RET // 2, 1) * sin2


def _any_spec():
    return pl.BlockSpec(memory_space=pl.ANY)


def _norm_cast_body(x_ref, g_ref, o_ref):
    o_ref[...] = _rms(x_ref[...], g_ref[...]).astype(BF16)


def _norm_cast(h, g_all, layer):
    t = h.shape[0]
    tm = 1024
    return pl.pallas_call(
        _norm_cast_body,
        grid=(t // tm,),
        in_specs=[pl.BlockSpec((tm, D_MODEL), lambda i: (i, 0)),
                  pl.BlockSpec((None, 1, D_MODEL), lambda i: (layer, 0, 0))],
        out_specs=pl.BlockSpec((tm, D_MODEL), lambda i: (i, 0)),
        out_shape=jax.ShapeDtypeStruct((t, D_MODEL), BF16),
        compiler_params=_params(("parallel",), 32),
        name="norm_cast",
    )(h, g_all)


def _inproj_body(x_ref, w_ref, o_ref, wb_ref, *, gates):
    @pl.when(pl.program_id(1) == 0)
    def _():
        wb_ref[...] = w_ref[...].astype(BF16)

    z = _dot(x_ref[...], wb_ref[...])
    o_ref[...] = jax.nn.sigmoid(z).astype(BF16) if gates else z


def _inproj(xn, w_all, layer, gates):
    t = xn.shape[0]
    tm, tn = 1024, 1536
    col0, cols, dtype = (MIX_COLS // tn, IN_COLS - MIX_COLS, BF16) if gates else (0, MIX_COLS, F32)
    return pl.pallas_call(
        functools.partial(_inproj_body, gates=gates),
        grid=(cols // tn, t // tm),
        in_specs=[
            pl.BlockSpec((tm, D_MODEL), lambda j, i: (i, 0)),
            pl.BlockSpec((None, D_MODEL, tn), lambda j, i: (layer, 0, col0 + j)),
        ],
        out_specs=pl.BlockSpec((tm, tn), lambda j, i: (i, j)),
        out_shape=jax.ShapeDtypeStruct((t, cols), dtype),
        scratch_shapes=[pltpu.VMEM((D_MODEL, tn), BF16)],
        compiler_params=_params(("parallel", "arbitrary"), 48),
        name="inproj_gates" if gates else "inproj",
    )(xn, w_all)


def _mix_prompt_body(u_ref, v_ref, q_ref, k_ref, vr_ref, gr_ref, xc_ref, cos_ref, sin_ref,
                     lng_ref, lnb_ref, ws_ref, bmap_ref, din_ref, dq_ref, dk_ref, dc_ref,
                     wp_ref, ps_ref,
                     a_ref, o_ref, c_ref, s_out_ref, buf_out_ref,
                     s_scr, xx_scr):
    n = pl.program_id(1)
    c = CHUNK
    hist = 2 * SUBLANES

    @pl.when(n == 0)
    def _():
        s_scr[...] = jnp.zeros_like(s_scr)
        xx_scr[0:hist, :] = jnp.zeros((hist, D_C), F32)

    row = lax.broadcasted_iota(I32, (c, c), 0)
    col = lax.broadcasted_iota(I32, (c, c), 1)

    u = jax.nn.gelu(u_ref[...])
    v = _layer_norm(jax.nn.gelu(v_ref[...]), lng_ref[...], lnb_ref[...])
    for g in range(G_A):
        sl = slice(g * DG_A, (g + 1) * DG_A)
        w = jnp.where(col <= row, ws_ref[g], 0.0).astype(BF16)
        mixed = _dot(w, v[:, sl].astype(BF16)) + bmap_ref[:, sl]
        a_ref[:, sl] = (u[:, sl] * mixed).astype(BF16)

    cos2 = cos_ref[...]
    sin2 = sin_ref[...]
    for h in range(H_RET):
        qs = slice(h * DK_RET, (h + 1) * DK_RET)
        vs = slice(h * DV_RET, (h + 1) * DV_RET)
        qr = _rotary(q_ref[:, qs], cos2, sin2)
        kr = _rotary(k_ref[:, qs], cos2, sin2) * (DK_RET ** -0.5)
        qb = qr.astype(BF16)
        vb = vr_ref[:, vs].astype(BF16)
        scores = _dot_nt(qb, kr.astype(BF16)) * din_ref[h]
        inner = _dot(scores.astype(BF16), vb)
        s_old = s_scr[h]
        cross = _dot(qb, s_old.astype(BF16)) * dq_ref[h]
        kd = (kr * dk_ref[h]).astype(BF16)
        s_scr[h] = dc_ref[h] * s_old + _dot_tn(kd, vb)
        oh = inner + cross
        oh = oh * lax.rsqrt(jnp.mean(oh * oh, axis=-1, keepdims=True) + EPS)
        o_ref[:, vs] = (jax.nn.silu(gr_ref[:, vs]) * oh).astype(BF16)

    xc = xc_ref[...]
    xx_scr[hist:hist + c, :] = xc
    pos1 = n * c + 1 + lax.broadcasted_iota(I32, (c, DG_C), 0)
    for gi, w in enumerate(POOL_WINDOWS):
        sl = slice(gi * DG_C, (gi + 1) * DG_C)
        acc = xx_scr[hist:hist + c, sl]
        for j in range(1, w):
            acc = acc + xx_scr[hist - j:hist - j + c, sl]
        cnt = jnp.minimum(w, pos1).astype(F32)
        pooled = acc / cnt - xc[:, sl]
        c_ref[:, sl] = (_dot(pooled.astype(BF16), wp_ref[gi]) * ps_ref[:, sl]).astype(BF16)
    xx_scr[0:hist, :] = xx_scr[c:c + hist, :]

    @pl.when(n == pl.num_programs(1) - 1)
    def _():
        s_out_ref[...] = s_scr[...]
        buf_out_ref[...] = xx_scr[hist - POOL_BUF:hist, :]


def _mix_prompt(z, batch, seq, tabs, lw, layer):
    c = CHUNK
    nc = seq // c
    t = z.shape[0]

    def zspec(width, blk):
        return pl.BlockSpec((c, width), lambda b, n: (b * nc + n, blk))

    def whole(shape):
        nd = len(shape)
        return pl.BlockSpec((None,) + shape, lambda b, n: (layer,) + (0,) * nd)

    def const(shape):
        nd = len(shape)
        return pl.BlockSpec(shape, lambda b, n: (0,) * nd)

    in_specs = [
        zspec(D_A, BLK_U), zspec(D_A, BLK_V), zspec(QK_W, BLK_Q), zspec(QK_W, BLK_K),
        zspec(V_W, BLK_VR), zspec(V_W, BLK_GR), zspec(D_C, BLK_XC),
        pl.BlockSpec((c, DK_RET), lambda b, n: (n, 0)),
        pl.BlockSpec((c, DK_RET), lambda b, n: (n, 0)),
        whole((1, D_A)), whole((1, D_A)), whole((G_A, c, c)), whole((c, D_A)),
        const((H_RET, c, c)), const((H_RET, c, 1)), const((H_RET, c, 1)), const((H_RET, 1, 1)),
        whole((len(POOL_WINDOWS), DG_C, DG_C)), whole((1, D_C)),
    ]
    out_specs = [
        pl.BlockSpec((c, D_A), lambda b, n: (b * nc + n, 0)),
        pl.BlockSpec((c, V_W), lambda b, n: (b * nc + n, 0)),
        pl.BlockSpec((c, D_C), lambda b, n: (b * nc + n, 0)),
        pl.BlockSpec((None, H_RET, DK_RET, DV_RET), lambda b, n: (b, 0, 0, 0)),
        pl.BlockSpec((None, POOL_BUF, D_C), lambda b, n: (b, 0, 0)),
    ]
    out_shape = [
        jax.ShapeDtypeStruct((t, D_A), BF16),
        jax.ShapeDtypeStruct((t, V_W), BF16),
        jax.ShapeDtypeStruct((t, D_C), BF16),
        jax.ShapeDtypeStruct((batch, H_RET, DK_RET, DV_RET), F32),
        jax.ShapeDtypeStruct((batch, POOL_BUF, D_C), F32),
    ]
    return pl.pallas_call(
        _mix_prompt_body,
        grid=(batch, nc),
        in_specs=in_specs,
        out_specs=out_specs,
        out_shape=out_shape,
        scratch_shapes=[pltpu.VMEM((H_RET, DK_RET, DV_RET), F32),
                        pltpu.VMEM((c + 2 * SUBLANES, D_C), F32)],
        compiler_params=_params(("arbitrary", "arbitrary"), 32),
        name="mix_prompt",
    )(z, z, z, z, z, z, z, tabs["cos"], tabs["sin"],
      lw["ln_g"], lw["ln_b"], lw["w_spatial"], lw["bmap_p"],
      tabs["din"], tabs["dq"], tabs["dk"], tabs["dc"],
      lw["w_pool"], lw["pool_scale"])


SAMPLE_NB = 8


def _mix_sample_body(u_ref, v_ref, q_ref, k_ref, vr_ref, gr_ref, xc_ref, cos_ref, sin_ref,
                     lng_ref, lnb_ref, wmix_ref, bmap_ref, din_ref, dq_ref, dk_ref, dc_ref,
                     wp_ref, ps_ref, s_in_ref, buf_in_ref, *rest, seq, pos0, n_alias):
    a_ref, o_ref, c_ref, vn_ref, s_out_ref, buf_out_ref, xx_scr = rest[n_alias:]
    nb = SAMPLE_NB
    r = nb * seq
    hist = 2 * SUBLANES

    u = jax.nn.gelu(u_ref[...])
    v = _layer_norm(jax.nn.gelu(v_ref[...]), lng_ref[...], lnb_ref[...])
    vn_ref[...] = v
    v3 = v.reshape(nb, seq, D_A)
    t_idx = lax.broadcasted_iota(I32, (seq, D_A), 0)
    mixed = jnp.broadcast_to(bmap_ref[...][None], (nb, seq, D_A))
    for s in range(seq):
        w_s = jnp.where(t_idx >= s, wmix_ref[s], 0.0)
        mixed = mixed + w_s[None] * v3[:, s:s + 1, :]
    a_ref[...] = (u * mixed.reshape(r, D_A)).astype(BF16)

    cos2 = cos_ref[...]
    sin2 = sin_ref[...]
    rowb = lax.broadcasted_iota(I32, (r, DV_RET), 0) // seq
    for h in range(H_RET):
        qs = slice(h * DK_RET, (h + 1) * DK_RET)
        vs = slice(h * DV_RET, (h + 1) * DV_RET)
        qr = _rotary(q_ref[:, qs], cos2, sin2)
        kr = _rotary(k_ref[:, qs], cos2, sin2) * (DK_RET ** -0.5)
        qb = qr.astype(BF16)
        vh = vr_ref[:, vs]
        scores = _dot_nt(qb, kr.astype(BF16)) * din_ref[h]
        inner = _dot(scores.astype(BF16), vh.astype(BF16))
        kd = (kr * dk_ref[h]).astype(BF16)
        cross = jnp.zeros((r, DV_RET), F32)
        for b in range(nb):
            s_old = s_in_ref[b, h]
            cross = jnp.where(rowb == b, _dot(qb, s_old.astype(BF16)), cross)
            vm = jnp.where(rowb == b, vh, 0.0).astype(BF16)
            s_out_ref[b, h] = dc_ref[h] * s_old + _dot_tn(kd, vm)
        oh = inner + cross * dq_ref[h]
        oh = oh * lax.rsqrt(jnp.mean(oh * oh, axis=-1, keepdims=True) + EPS)
        o_ref[:, vs] = (jax.nn.silu(gr_ref[:, vs]) * oh).astype(BF16)

    xc = xc_ref[...]
    xx_scr[:, hist - POOL_BUF:hist, :] = buf_in_ref[...]
    xx_scr[:, hist:hist + seq, :] = xc.reshape(nb, seq, D_C)
    pos1 = pos0 + 1 + lax.broadcasted_iota(I32, (seq, DG_C), 0)
    for gi, w in enumerate(POOL_WINDOWS):
        sl = slice(gi * DG_C, (gi + 1) * DG_C)
        acc = xx_scr[:, hist:hist + seq, sl]
        for j in range(1, w):
            acc = acc + xx_scr[:, hist - j:hist - j + seq, sl]
        cnt = jnp.minimum(w, pos1).astype(F32)
        pooled = (acc / cnt[None]).reshape(r, DG_C) - xc[:, sl]
        c_ref[:, sl] = (_dot(pooled.astype(BF16), wp_ref[gi]) * ps_ref[:, sl]).astype(BF16)
    buf_out_ref[...] = xx_scr[:, hist + seq - POOL_BUF:hist + seq, :]


def _mix_sample(z, a_buf, o_buf, c_buf, ret_buf, row0, batch, seq, pos0, tabs, lw,
                state_ret, state_pool, layer):
    nb = SAMPLE_NB
    r = nb * seq
    blk0 = row0 // r
    rows = batch * seq
    t = z.shape[0]

    def zspec(width, blk):
        return pl.BlockSpec((r, width), lambda i: (blk0 + i, blk))

    def whole(shape):
        nd = len(shape)
        return pl.BlockSpec((None,) + shape, lambda i: (layer,) + (0,) * nd)

    def const(shape):
        nd = len(shape)
        return pl.BlockSpec(shape, lambda i: (0,) * nd)

    in_specs = [
        zspec(D_A, BLK_U), zspec(D_A, BLK_V), zspec(QK_W, BLK_Q), zspec(QK_W, BLK_K),
        zspec(V_W, BLK_VR), zspec(V_W, BLK_GR), zspec(D_C, BLK_XC),
        const((r, DK_RET)), const((r, DK_RET)),
        whole((1, D_A)), whole((1, D_A)), whole((seq, seq, D_A)), whole((seq, D_A)),
        const((H_RET, r, r)), const((H_RET, r, 1)), const((H_RET, r, 1)), const((H_RET, 1, 1)),
        whole((len(POOL_WINDOWS), DG_C, DG_C)), whole((1, D_C)),
        pl.BlockSpec((None, nb, H_RET, DK_RET, DV_RET), lambda i: (layer, i, 0, 0, 0)),
        pl.BlockSpec((None, nb, POOL_BUF, D_C), lambda i: (layer, i, 0, 0)),
    ]
    args = [z, z, z, z, z, z, z, tabs["cos"], tabs["sin"],
            lw["ln_g"], lw["ln_b"], lw["wmix_s"], lw["bmap_s"],
            tabs["din"], tabs["dq"], tabs["dk"], tabs["dc"],
            lw["w_pool"], lw["pool_scale"], state_ret, state_pool]
    n_in = len(args)
    aliased = [a_buf, o_buf, c_buf] + ([ret_buf] if ret_buf is not None else [])
    aliases = {n_in + 0: 0, n_in + 1: 1, n_in + 2: 2}
    if ret_buf is not None:
        aliases[n_in + 3] = 4
    in_specs += [_any_spec()] * len(aliased)
    out_specs = [
        pl.BlockSpec((r, D_A), lambda i: (blk0 + i, 0)),
        pl.BlockSpec((r, V_W), lambda i: (blk0 + i, 0)),
        pl.BlockSpec((r, D_C), lambda i: (blk0 + i, 0)),
        pl.BlockSpec((r, D_A), lambda i: (i, 0)),
        pl.BlockSpec((None, nb, H_RET, DK_RET, DV_RET), lambda i: (layer, i, 0, 0, 0)),
        pl.BlockSpec((nb, POOL_BUF, D_C), lambda i: (i, 0, 0)),
    ]
    out_shape = [
        jax.ShapeDtypeStruct((t, D_A), BF16),
        jax.ShapeDtypeStruct((t, V_W), BF16),
        jax.ShapeDtypeStruct((t, D_C), BF16),
        jax.ShapeDtypeStruct((rows, D_A), F32),
        jax.ShapeDtypeStruct((DEPTH, batch, H_RET, DK_RET, DV_RET), F32),
        jax.ShapeDtypeStruct((batch, POOL_BUF, D_C), F32),
    ]
    return pl.pallas_call(
        functools.partial(_mix_sample_body, seq=seq, pos0=pos0, n_alias=len(aliased)),
        grid=(batch // nb,),
        in_specs=in_specs,
        out_specs=out_specs,
        out_shape=out_shape,
        scratch_shapes=[pltpu.VMEM((nb, 2 * SUBLANES + seq, D_C), F32)],
        input_output_aliases=aliases,
        compiler_params=_params(("parallel",), 40),
        name="mix_sample",
    )(*args, *aliased)


def _merge_body(a_ref, o_ref, c_ref, sa_ref, sb_ref, sc_ref, h_ref,
                wa_ref, wb_ref, wc_ref, wo_ref, out_ref):
    m = sa_ref[...].astype(F32) * _dot(a_ref[...], wa_ref[...])
    m = m + sb_ref[...].astype(F32) * _dot(o_ref[...], wb_ref[...])
    m = m + sc_ref[...].astype(F32) * _dot(c_ref[...], wc_ref[...])
    out_ref[...] = h_ref[...] + _dot(m.astype(BF16), wo_ref[...])


def _merge(a, o, c, sg, h, lw, layer):
    t = h.shape[0]
    tm = 512

    def rows(width, blk=0):
        return pl.BlockSpec((tm, width), lambda i: (i, blk))

    def whole(shape):
        nd = len(shape)
        return pl.BlockSpec((None,) + shape, lambda i: (layer,) + (0,) * nd)

    return pl.pallas_call(
        _merge_body,
        grid=(t // tm,),
        in_specs=[rows(D_A), rows(V_W), rows(D_C),
                  rows(D_MODEL, 0), rows(D_MODEL, 1), rows(D_MODEL, 2),
                  rows(D_MODEL),
                  whole((D_A, D_MODEL)), whole((V_W, D_MODEL)), whole((D_C, D_MODEL)),
                  whole((D_MODEL, D_MODEL))],
        out_specs=rows(D_MODEL),
        out_shape=jax.ShapeDtypeStruct((t, D_MODEL), F32),
        compiler_params=_params(("parallel",), 48),
        name="merge",
    )(a, o, c, sg, sg, sg, h, lw["w_br_a"], lw["w_br_b"], lw["w_br_c"], lw["w_out"])


def _ffn_body(x_ref, g_ref, gn_ref, wg_ref, wu_ref, wd_ref, out_ref, xnext_ref, xn_ref, acc_ref):
    j = pl.program_id(1)

    @pl.when(j == 0)
    def _():
        xn_ref[...] = _rms(x_ref[...], g_ref[...]).astype(BF16)
        acc_ref[...] = jnp.zeros_like(acc_ref)

    xn = xn_ref[...]
    hid = jax.nn.silu(_dot(xn, wg_ref[...])) * _dot(xn, wu_ref[...])
    acc_ref[...] += _dot(hid.astype(BF16), wd_ref[...])

    @pl.when(j == pl.num_programs(1) - 1)
    def _():
        out = x_ref[...] + acc_ref[...]
        out_ref[...] = out
        xnext_ref[...] = _rms(out, gn_ref[...]).astype(BF16)


def _ffn_dense(h, g_all, g_next_all, wg, wu, wd, layer, j):
    t = h.shape[0]
    tm, tf = 512, 1408
    return pl.pallas_call(
        _ffn_body,
        grid=(t // tm, D_FF // tf),
        in_specs=[
            pl.BlockSpec((tm, D_MODEL), lambda i, f: (i, 0)),
            pl.BlockSpec((None, 1, D_MODEL), lambda i, f: (layer, 0, 0)),
            pl.BlockSpec((None, 1, D_MODEL), lambda i, f: (layer + 1, 0, 0)),
            pl.BlockSpec((None, D_MODEL, tf), lambda i, f: (j, 0, f)),
            pl.BlockSpec((None, D_MODEL, tf), lambda i, f: (j, 0, f)),
            pl.BlockSpec((None, tf, D_MODEL), lambda i, f: (j, f, 0)),
        ],
        out_specs=[pl.BlockSpec((tm, D_MODEL), lambda i, f: (i, 0)),
                   pl.BlockSpec((tm, D_MODEL), lambda i, f: (i, 0))],
        out_shape=[jax.ShapeDtypeStruct((t, D_MODEL), F32),
                   jax.ShapeDtypeStruct((t, D_MODEL), BF16)],
        scratch_shapes=[pltpu.VMEM((tm, D_MODEL), BF16), pltpu.VMEM((tm, D_MODEL), F32)],
        compiler_params=_params(("parallel", "arbitrary"), 48),
        name="ffn_dense",
    )(h, g_all, g_next_all, wg, wu, wd)


def _router_body(x_ref, g_ref, wr_ref, idx_ref, wgt_ref, cnt_ref, carry_scr):
    @pl.when(pl.program_id(0) == 0)
    def _():
        carry_scr[...] = jnp.zeros_like(carry_scr)

    tm = x_ref.shape[0]
    xn = _rms(x_ref[...], g_ref[...])
    logits = jnp.dot(xn, wr_ref[...], preferred_element_type=F32, precision=lax.Precision.HIGHEST)
    lane = lax.broadcasted_iota(I32, logits.shape, 1)
    neg = jnp.float32(-jnp.inf)
    logits = jnp.where(lane < N_EXPERTS, logits, neg)
    v1 = jnp.max(logits, axis=-1, keepdims=True)
    i1 = jnp.min(jnp.where(logits == v1, lane, LANES), axis=-1, keepdims=True)
    rest = jnp.where(lane == i1, neg, logits)
    v2 = jnp.max(rest, axis=-1, keepdims=True)
    i2 = jnp.min(jnp.where(rest == v2, lane, LANES), axis=-1, keepdims=True)
    e2 = jnp.exp(v2 - v1)
    den = 1.0 + e2
    w1 = 1.0 / den
    w2 = e2 / den

    hit1 = lane == i1
    hit2 = lane == i2
    cnt = jnp.where(hit1, 1.0, jnp.where(hit2, 1.0, 0.0))
    r_i = lax.broadcasted_iota(I32, (tm, tm), 0)
    c_i = lax.broadcasted_iota(I32, (tm, tm), 1)
    strictly_lower = jnp.where(c_i < r_i, 1.0, 0.0).astype(BF16)
    before = _dot(strictly_lower, cnt.astype(BF16)) + carry_scr[...]
    rank1 = jnp.sum(jnp.where(hit1, before, 0.0), axis=-1, keepdims=True).astype(I32)
    rank2 = jnp.sum(jnp.where(hit2, before, 0.0), axis=-1, keepdims=True).astype(I32)
    carry_scr[...] += jnp.sum(cnt, axis=0, keepdims=True)

    idx_ref[...] = jnp.where(lane == 0, i1, jnp.where(lane == 1, i2,
                             jnp.where(lane == 2, rank1, jnp.where(lane == 3, rank2, 0))))
    wgt_ref[...] = jnp.where(lane == 0, w1, jnp.where(lane == 1, w2, 0.0))
    cnt_ref[...] = carry_scr[...].astype(I32)


def _router(h, g_all, wr_all, layer, j):
    t = h.shape[0]
    tm = ROW_TILE
    return pl.pallas_call(
        _router_body,
        grid=(t // tm,),
        in_specs=[
            pl.BlockSpec((tm, D_MODEL), lambda i: (i, 0)),
            pl.BlockSpec((None, 1, D_MODEL), lambda i: (layer, 0, 0)),
            pl.BlockSpec((None, D_MODEL, LANES), lambda i: (j, 0, 0)),
        ],
        out_specs=[pl.BlockSpec((tm, LANES), lambda i: (i, 0)),
                   pl.BlockSpec((tm, LANES), lambda i: (i, 0)),
                   pl.BlockSpec((1, LANES), lambda i: (0, 0))],
        out_shape=[jax.ShapeDtypeStruct((t, LANES), I32),
                   jax.ShapeDtypeStruct((t, LANES), F32),
                   jax.ShapeDtypeStruct((1, LANES), I32)],
        scratch_shapes=[pltpu.VMEM((1, LANES), F32)],
        compiler_params=_params(("arbitrary",), 32),
        name="router",
    )(h, g_all, wr_all)


def _routing_plan(idx, cnt, n_tiles):
    e1, e2, r1, r2 = idx[:, 0], idx[:, 1], idx[:, 2], idx[:, 3]
    counts = cnt[0, :N_EXPERTS]
    gsz = ((counts + MOE_TILE - 1) // MOE_TILE) * MOE_TILE
    gend = jnp.cumsum(gsz)
    gstart = gend - gsz
    pos1 = gstart[e1] + r1
    pos2 = gstart[e2] + r2
    n_used = gend[-1] // MOE_TILE
    tile_ids = jnp.arange(n_tiles, dtype=I32)
    texp = jnp.sum((tile_ids * MOE_TILE)[:, None] >= gend[None, :], axis=1).astype(I32)
    texp = jnp.minimum(texp, N_EXPERTS - 1)
    texp = jnp.where(tile_ids < n_used, texp, texp[jnp.maximum(n_used - 1, 0)])
    return pos1.astype(I32), pos2.astype(I32), texp, n_used.astype(I32).reshape(1), gend.astype(I32)


TOKEN_ROWS = D_MODEL // LANES
assert TOKEN_ROWS == SUBLANES


def _token_tile(ref, token):
    return ref.at[pl.ds(pl.multiple_of(token * TOKEN_ROWS, TOKEN_ROWS), TOKEN_ROWS)]


def _store_token_major(dst_ref, x):
    n = x.shape[0]
    for c in range(TOKEN_ROWS):
        dst_ref[pl.ds(c, n, stride=TOKEN_ROWS), :] = x[:, c * LANES:(c + 1) * LANES]


def _load_token_major(src_ref, n):
    return jnp.concatenate([src_ref[pl.ds(c, n, stride=TOKEN_ROWS), :] for c in range(TOKEN_ROWS)],
                           axis=1)


def _dispatch_body(pos1_ref, pos2_ref, gend_ref, h_ref, g_ref, xs_ref, hn_scr, sems):
    tm = h_ref.shape[0]
    i = pl.program_id(0)
    base = i * tm
    step_rows = tm * TOKEN_ROWS
    slot = i % 2
    buf = hn_scr.at[slot]

    @pl.when(i == 0)
    def _():
        zeros = hn_scr.at[1]
        zeros[...] = jnp.zeros(zeros.shape, F32)
        for e in range(N_EXPERTS):
            end = gend_ref[e]
            start = gend_ref[e - 1] if e else 0

            @pl.when(end > start)
            def _():
                for part in range(MOE_TILE // tm):
                    row0 = pl.multiple_of((end - MOE_TILE + part * tm) * TOKEN_ROWS, TOKEN_ROWS)
                    cp = pltpu.make_async_copy(zeros, xs_ref.at[pl.ds(row0, step_rows)], sems.at[1, 0])
                    cp.start()
                    cp.wait()

    _store_token_major(buf, _rms(h_ref[...], g_ref[...]))

    def start(t, carry):
        src = _token_tile(buf, t)
        for k, pos_ref in enumerate((pos1_ref, pos2_ref)):
            pltpu.make_async_copy(src, _token_tile(xs_ref, pos_ref[base + t]),
                                  sems.at[slot, k]).start(priority=k)
        return carry

    lax.fori_loop(0, tm, start, 0, unroll=8)

    def drain(s):
        for k in range(TOP_K):
            pltpu.make_async_copy(hn_scr.at[s], xs_ref.at[pl.ds(0, step_rows)], sems.at[s, k]).wait()

    @pl.when(i > 0)
    def _():
        drain(1 - slot)

    @pl.when(i == pl.num_programs(0) - 1)
    def _():
        drain(slot)


def _dispatch(h, g_all, pos1, pos2, gend, n_rows, layer):
    t = h.shape[0]
    tm = ROW_TILE
    return pl.pallas_call(
        _dispatch_body,
        grid_spec=pltpu.PrefetchScalarGridSpec(
            num_scalar_prefetch=3,
            grid=(t // tm,),
            in_specs=[pl.BlockSpec((tm, D_MODEL), lambda i, *_: (i, 0)),
                      pl.BlockSpec((None, 1, D_MODEL), lambda i, *_: (layer, 0, 0))],
            out_specs=_any_spec(),
            scratch_shapes=[pltpu.VMEM((2, tm * TOKEN_ROWS, LANES), F32),
                            pltpu.SemaphoreType.DMA((2, TOP_K))],
        ),
        out_shape=jax.ShapeDtypeStruct((n_rows * TOKEN_ROWS, LANES), F32),
        compiler_params=_params(("arbitrary",), 32),
        name="moe_dispatch",
    )(pos1, pos2, gend, h, g_all)


def _moe_group_body(texp_ref, nused_ref, x_ref, wg_ref, wu_ref, wd_ref, y_ref, xb_ref, acc_ref):
    i = pl.program_id(0)
    f = pl.program_id(1)
    tm = xb_ref.shape[0]

    @pl.when(i < nused_ref[0])
    def _():
        @pl.when(f == 0)
        def _():
            xb_ref[...] = _load_token_major(x_ref, tm).astype(BF16)
            acc_ref[...] = jnp.zeros_like(acc_ref)

        xb = xb_ref[...]
        hid = jax.nn.silu(_dot(xb, wg_ref[...].astype(BF16))) * _dot(xb, wu_ref[...].astype(BF16))
        acc_ref[...] += _dot(hid.astype(BF16), wd_ref[...].astype(BF16))

        @pl.when(f == pl.num_programs(1) - 1)
        def _():
            _store_token_major(y_ref, acc_ref[...])


def _moe_group(xs, texp, n_used, wg, wu, wd, j):
    n_rows = xs.shape[0] // TOKEN_ROWS
    tm, tf = MOE_TILE, 512
    nf = D_FF_EXP // tf

    def row_blk(i, nu):
        return jnp.maximum(jnp.minimum(i, nu[0] - 1), 0)

    def f_blk(i, f, nu):
        return jnp.where(i < nu[0], f, nf - 1)

    return pl.pallas_call(
        _moe_group_body,
        grid_spec=pltpu.PrefetchScalarGridSpec(
            num_scalar_prefetch=2,
            grid=(n_rows // tm, nf),
            in_specs=[
                pl.BlockSpec((tm * TOKEN_ROWS, LANES), lambda i, f, te, nu: (row_blk(i, nu), 0)),
                pl.BlockSpec((None, None, D_MODEL, tf), lambda i, f, te, nu: (j, te[i], 0, f_blk(i, f, nu))),
                pl.BlockSpec((None, None, D_MODEL, tf), lambda i, f, te, nu: (j, te[i], 0, f_blk(i, f, nu))),
                pl.BlockSpec((None, None, tf, D_MODEL), lambda i, f, te, nu: (j, te[i], f_blk(i, f, nu), 0)),
            ],
            out_specs=pl.BlockSpec((tm * TOKEN_ROWS, LANES), lambda i, f, te, nu: (row_blk(i, nu), 0)),
            scratch_shapes=[pltpu.VMEM((tm, D_MODEL), BF16), pltpu.VMEM((tm, D_MODEL), F32)],
        ),
        out_shape=jax.ShapeDtypeStruct((n_rows * TOKEN_ROWS, LANES), F32),
        compiler_params=_params(("arbitrary", "arbitrary"), 52),
        name="moe_group",
    )(texp, n_used, xs, wg, wu, wd)


def _combine_body(pos1_ref, pos2_ref, h_ref, wgt_ref, gn_ref, ys_ref, *rest, last):
    *out_refs, y_scr, sems = rest
    tm = h_ref.shape[0]
    i = pl.program_id(0)
    slot = i % 2

    def fetch(step, s):
        base = step * tm

        def start(t, carry):
            for k, pos_ref in enumerate((pos1_ref, pos2_ref)):
                pltpu.make_async_copy(_token_tile(ys_ref, pos_ref[base + t]),
                                      _token_tile(y_scr.at[s, k], t), sems.at[s, k]).start(priority=k)
            return carry

        lax.fori_loop(0, tm, start, 0, unroll=8)

    @pl.when(i == 0)
    def _():
        fetch(0, 0)

    @pl.when(i + 1 < pl.num_programs(0))
    def _():
        fetch(i + 1, 1 - slot)

    for k in range(TOP_K):
        pltpu.make_async_copy(ys_ref.at[pl.ds(0, tm * TOKEN_ROWS)], y_scr.at[slot, k],
                              sems.at[slot, k]).wait()
    w = wgt_ref[...]
    out = h_ref[...] + (w[:, 0:1] * _load_token_major(y_scr.at[slot, 0], tm)
                        + w[:, 1:2] * _load_token_major(y_scr.at[slot, 1], tm))
    if last:
        out_refs[0][...] = _rms(out, gn_ref[...])
    else:
        out_refs[0][...] = out
        out_refs[1][...] = _rms(out, gn_ref[...]).astype(BF16)


def _combine(h, wgt, g_next, ys, pos1, pos2, last):
    t = h.shape[0]
    tm = ROW_TILE
    rows = pl.BlockSpec((tm, D_MODEL), lambda i, *_: (i, 0))
    full = jax.ShapeDtypeStruct((t, D_MODEL), F32)
    return pl.pallas_call(
        functools.partial(_combine_body, last=last),
        grid_spec=pltpu.PrefetchScalarGridSpec(
            num_scalar_prefetch=2,
            grid=(t // tm,),
            in_specs=[rows,
                      pl.BlockSpec((tm, LANES), lambda i, *_: (i, 0)),
                      pl.BlockSpec((1, D_MODEL), lambda i, *_: (0, 0)),
                      _any_spec()],
            out_specs=[rows] if last else [rows, rows],
            scratch_shapes=[pltpu.VMEM((2, TOP_K, tm * TOKEN_ROWS, LANES), F32),
                            pltpu.SemaphoreType.DMA((2, TOP_K))],
        ),
        out_shape=[full] if last else [full, jax.ShapeDtypeStruct((t, D_MODEL), BF16)],
        compiler_params=_params(("arbitrary",), 32),
        name="moe_combine",
    )(pos1, pos2, h, wgt, g_next, ys)


def _ffn_moe(h, g_all, g_next, wr_all, wg, wu, wd, layer, j, last):
    t = h.shape[0]
    n_rows = TOP_K * t + N_EXPERTS * MOE_TILE
    idx, wgt, cnt = _router(h, g_all, wr_all, layer, j)
    pos1, pos2, texp, n_used, gend = _routing_plan(idx, cnt, n_rows // MOE_TILE)
    xs = _dispatch(h, g_all, pos1, pos2, gend, n_rows, layer)
    ys = _moe_group(xs, texp, n_used, wg, wu, wd, j)
    return _combine(h, wgt, g_next, ys, pos1, pos2, last)


def _rope_tables(pos):
    half = DK_RET // 2
    inv = ROPE_BASE ** (-jnp.arange(half, dtype=F32) / half)
    ang = pos.astype(F32)[:, None] * inv[None, :]
    cos, sin = jnp.cos(ang), jnp.sin(ang)
    return jnp.concatenate([cos, cos], axis=-1), jnp.concatenate([-sin, sin], axis=-1)


def _decay_tables(c):
    log_g = jnp.log1p(-jnp.exp2(-5.0 - jnp.arange(H_RET, dtype=F32)))
    idx = jnp.arange(c, dtype=F32)
    diff = idx[:, None] - idx[None, :]
    causal = diff >= 0
    d_in = jnp.where(causal[None], jnp.exp(log_g[:, None, None] * jnp.where(causal, diff, 0.0)[None]), 0.0)
    d_q = jnp.exp(log_g[:, None] * (idx[None, :] + 1.0))
    d_k = jnp.exp(log_g[:, None] * (c - 1.0 - idx[None, :]))
    d_c = jnp.exp(log_g * c)
    return d_in, d_q, d_k, d_c


def _prompt_tables(seq):
    c = min(CHUNK, seq)
    cos2, sin2 = _rope_tables(jnp.arange(seq, dtype=I32))
    d_in, d_q, d_k, d_c = _decay_tables(c)
    return dict(cos=cos2, sin=sin2, din=d_in, dq=d_q[:, :, None], dk=d_k[:, :, None],
                dc=d_c[:, None, None])


def _sample_tables(seq, pos0):
    nb = SAMPLE_NB
    cos2, sin2 = _rope_tables(pos0 + jnp.arange(seq, dtype=I32))
    d_in, d_q, d_k, d_c = _decay_tables(seq)
    eye = jnp.eye(nb, dtype=F32)
    d_blk = jnp.einsum("ab,hij->haibj", eye, d_in).reshape(H_RET, nb * seq, nb * seq)
    return dict(cos=jnp.tile(cos2, (nb, 1)), sin=jnp.tile(sin2, (nb, 1)), din=d_blk,
                dq=jnp.tile(d_q, (1, nb))[:, :, None], dk=jnp.tile(d_k, (1, nb))[:, :, None],
                dc=d_c[:, None, None])


def kernel(x_prompt, x_sample, state_ret, state_pool, norm1_g, w_in, ln_v_g, ln_v_b, w_spatial, b_spatial, w_pool, pool_scale, w_br_a, w_br_b, w_br_c, w_out, norm2_g, w_ffn_gate, w_ffn_up, w_ffn_down, w_router, w_exp_gate, w_exp_up, w_exp_down, final_norm_g):
    batch, seq, _ = x_prompt.shape
    dec_batch, dec_seq, _ = x_sample.shape
    n_prompt = batch * seq
    n_sample = dec_batch * dec_seq
    assert seq % CHUNK == 0 and dec_seq <= CHUNK and PAST_LEN % CHUNK == 0
    assert n_prompt % (SAMPLE_NB * dec_seq) == 0 and dec_batch % SAMPLE_NB == 0

    lw = dict(
        ln_g=ln_v_g[:, None, :], ln_b=ln_v_b[:, None, :],
        w_spatial=w_spatial,
        bmap_p=jnp.repeat(jnp.swapaxes(b_spatial, 1, 2), DG_A, axis=2),
        wmix_s=jnp.repeat(jnp.transpose(w_spatial[:, :, :dec_seq, :dec_seq], (0, 3, 2, 1)), DG_A, axis=3),
        w_pool=w_pool.astype(BF16), pool_scale=pool_scale[:, None, :],
        w_br_a=w_br_a.astype(BF16), w_br_b=w_br_b.astype(BF16), w_br_c=w_br_c.astype(BF16),
        w_out=w_out.astype(BF16),
    )
    lw["bmap_s"] = lw["bmap_p"][:, :dec_seq, :]
    g1 = norm1_g[:, None, :]
    g2 = norm2_g[:, None, :]
    wfg, wfu, wfd = w_ffn_gate.astype(BF16), w_ffn_up.astype(BF16), w_ffn_down.astype(BF16)
    wr = jnp.pad(w_router, ((0, 0), (0, 0), (0, LANES - N_EXPERTS)))
    tabs_p = _prompt_tables(seq)
    tabs_s = _sample_tables(dec_seq, PAST_LEN)

    h = jnp.concatenate([x_prompt.reshape(n_prompt, D_MODEL),
                         x_sample.reshape(n_sample, D_MODEL)], axis=0)
    ret_p, pool_p, pool_s, v_s = [], [], [], []
    ret_s = None
    assert DEPTH % 2 == 0
    xn = _norm_cast(h, g1, 0)
    for layer in range(DEPTH):
        j = layer // 2
        last = layer == DEPTH - 1
        z = _inproj(xn, w_in, layer, gates=False)
        sg = _inproj(xn, w_in, layer, gates=True)
        a, o, c, s_p, b_p = _mix_prompt(z, batch, seq, tabs_p, lw, layer)
        a, o, c, vn, ret_s, b_s = _mix_sample(z, a, o, c, ret_s, n_prompt, dec_batch, dec_seq, PAST_LEN,
                                              tabs_s, lw, state_ret, state_pool, layer)
        h = _merge(a, o, c, sg, h, lw, layer)
        if layer % 2 == 0:
            h, xn = _ffn_dense(h, g2, g1, wfg, wfu, wfd, layer, j)
        else:
            g_next = final_norm_g[None, :] if last else g1[layer + 1]
            res = _ffn_moe(h, g2, g_next, wr, w_exp_gate, w_exp_up, w_exp_down, layer, j, last)
            if last:
                y, = res
            else:
                h, xn = res
        ret_p.append(s_p)
        pool_p.append(b_p)
        pool_s.append(b_s)
        v_s.append(vn.reshape(dec_batch, dec_seq, D_A))

    y_prompt = y[:n_prompt].reshape(batch, seq, D_MODEL)
    y_sample = y[n_prompt:].reshape(dec_batch, dec_seq, D_MODEL)
    return (y_prompt, y_sample, jnp.stack(ret_p), ret_s, jnp.stack(pool_p),
            jnp.stack(pool_s), jnp.stack(v_s))
```

```python
import functools

import jax
import jax.numpy as jnp
from jax import lax
from jax.experimental import pallas as pl
from jax.experimental.pallas import tpu as pltpu

F32 = jnp.float32
BF16 = jnp.bfloat16
I32 = jnp.int32

D_MODEL = 1024
DEPTH = 4
PAST_LEN = 16384
D_A = D_MODEL // 2
CHUNK = 128
G_A = 4
DG_A = D_A // G_A
H_RET = 4
DK_RET = D_MODEL // 2 // H_RET
DV_RET = 2 * DK_RET
ROPE_BASE = 10000.0
D_C = D_MODEL // 2
POOL_WINDOWS = (2, 4, 8, 16)
DG_C = D_C // len(POOL_WINDOWS)
POOL_BUF = max(POOL_WINDOWS) - 1
D_FF = 2816
N_EXPERTS = 8
TOP_K = 2
D_FF_EXP = 3584
EPS = 1e-6
QK_W = H_RET * DK_RET
V_W = H_RET * DV_RET
IN_COLS = 2 * D_A + 2 * QK_W + 2 * V_W + D_C + 3 * D_MODEL

LANES = 128
SUBLANES = 8
MIB = 1024 * 1024

BLK_U, BLK_V, BLK_Q, BLK_K, BLK_XC = 0, 1, 2, 3, 8
BLK_VR, BLK_GR = 2, 3
MIX_COLS = 2 * D_A + 2 * QK_W + 2 * V_W + D_C

PROMPT_CHUNKS_PER_STEP = 2
MOE_TILE = 2048
MOE_SUB = 1024
ROW_TILE = 512


def _params(sem, vmem_mib):
    return pltpu.CompilerParams(dimension_semantics=sem, vmem_limit_bytes=vmem_mib * MIB)


def _rms(x, g):
    return x * lax.rsqrt(jnp.mean(x * x, axis=-1, keepdims=True) + EPS) * g


def _layer_norm(x, g, b):
    mu = jnp.mean(x, axis=-1, keepdims=True)
    xc = x - mu
    var = jnp.mean(xc * xc, axis=-1, keepdims=True)
    return xc * lax.rsqrt(var + EPS) * g + b


def _dot(a, b):
    return jnp.dot(a, b, preferred_element_type=F32)


def _dot_nt(a, b):
    return lax.dot_general(a, b, (((1,), (1,)), ((), ())), preferred_element_type=F32)


def _dot_tn(a, b):
    return lax.dot_general(a, b, (((0,), (0,)), ((), ())), preferred_element_type=F32)


def _rotary(x, cos2, sin2):
    return x * cos2 + pltpu.roll(x, DK_RET // 2, 1) * sin2


def _any_spec():
    return pl.BlockSpec(memory_space=pl.ANY)


def _norm_cast_body(x_ref, g_ref, o_ref):
    o_ref[...] = _rms(x_ref[...], g_ref[...]).astype(BF16)


def _norm_cast(h, g_all, layer):
    t = h.shape[0]
    tm = 1024
    return pl.pallas_call(
        _norm_cast_body,
        grid=(t // tm,),
        in_specs=[pl.BlockSpec((tm, D_MODEL), lambda i: (i, 0)),
                  pl.BlockSpec((None, 1, D_MODEL), lambda i: (layer, 0, 0))],
        out_specs=pl.BlockSpec((tm, D_MODEL), lambda i: (i, 0)),
        out_shape=jax.ShapeDtypeStruct((t, D_MODEL), BF16),
        compiler_params=_params(("parallel",), 32),
        name="norm_cast",
    )(h, g_all)


def _inproj_body(x_ref, w_ref, o_ref, wb_ref):
    @pl.when(pl.program_id(1) == 0)
    def _():
        wb_ref[...] = w_ref[...].astype(BF16)

    o_ref[...] = _dot(x_ref[...], wb_ref[...]).astype(o_ref.dtype)


def _inproj(xn, w_all, layer, gates):
    t = xn.shape[0]
    tm, tn = 1024, 1536
    col0, cols, dtype = (MIX_COLS // tn, IN_COLS - MIX_COLS, BF16) if gates else (0, MIX_COLS, F32)
    return pl.pallas_call(
        _inproj_body,
        grid=(cols // tn, t // tm),
        in_specs=[
            pl.BlockSpec((tm, D_MODEL), lambda j, i: (i, 0)),
            pl.BlockSpec((None, D_MODEL, tn), lambda j, i: (layer, 0, col0 + j)),
        ],
        out_specs=pl.BlockSpec((tm, tn), lambda j, i: (i, j)),
        out_shape=jax.ShapeDtypeStruct((t, cols), dtype),
        scratch_shapes=[pltpu.VMEM((D_MODEL, tn), BF16)],
        compiler_params=_params(("parallel", "arbitrary"), 48),
        name="inproj_gates" if gates else "inproj",
    )(xn, w_all)


def _mix_prompt_body(u_ref, v_ref, q_ref, k_ref, vr_ref, gr_ref, xc_ref, cos_ref, sin_ref,
                     lng_ref, lnb_ref, ws_ref, bmap_ref, din_ref, dq_ref, dk_ref, dc_ref,
                     wp_ref, ps_ref,
                     a_ref, o_ref, c_ref, s_out_ref, buf_out_ref,
                     s_scr, xx_scr):
    n = pl.program_id(1)
    c = CHUNK
    hist = 2 * SUBLANES

    @pl.when(n == 0)
    def _():
        s_scr[...] = jnp.zeros_like(s_scr)
        xx_scr[0:hist, :] = jnp.zeros((hist, D_C), F32)

    row = lax.broadcasted_iota(I32, (c, c), 0)
    col = lax.broadcasted_iota(I32, (c, c), 1)

    for sub in range(PROMPT_CHUNKS_PER_STEP):
        rs = slice(sub * c, (sub + 1) * c)

        u = jax.nn.gelu(u_ref[rs, :])
        v = _layer_norm(jax.nn.gelu(v_ref[rs, :]), lng_ref[...], lnb_ref[...])
        for g in range(G_A):
            sl = slice(g * DG_A, (g + 1) * DG_A)
            w = jnp.where(col <= row, ws_ref[g], 0.0).astype(BF16)
            mixed = _dot(w, v[:, sl].astype(BF16)) + bmap_ref[:, sl]
            a_ref[rs, sl] = (u[:, sl] * mixed).astype(BF16)

        cos2 = cos_ref[rs, :]
        sin2 = sin_ref[rs, :]
        for h in range(H_RET):
            qs = slice(h * DK_RET, (h + 1) * DK_RET)
            vs = slice(h * DV_RET, (h + 1) * DV_RET)
            qr = _rotary(q_ref[rs, qs], cos2, sin2)
            kr = _rotary(k_ref[rs, qs], cos2, sin2) * (DK_RET ** -0.5)
            qb = qr.astype(BF16)
            vb = vr_ref[rs, vs].astype(BF16)
            scores = _dot_nt(qb, kr.astype(BF16)) * din_ref[h]
            inner = _dot(scores.astype(BF16), vb)
            s_old = s_scr[h]
            cross = _dot(qb, s_old.astype(BF16)) * dq_ref[h]
            kd = (kr * dk_ref[h]).astype(BF16)
            s_scr[h] = dc_ref[h] * s_old + _dot_tn(kd, vb)
            oh = inner + cross
            oh = oh * lax.rsqrt(jnp.mean(oh * oh, axis=-1, keepdims=True) + EPS)
            o_ref[rs, vs] = (jax.nn.silu(gr_ref[rs, vs]) * oh).astype(BF16)

        xc = xc_ref[rs, :]
        xx_scr[hist:hist + c, :] = xc
        pos1 = (n * PROMPT_CHUNKS_PER_STEP + sub) * c + 1 + lax.broadcasted_iota(I32, (c, DG_C), 0)
        for gi, w in enumerate(POOL_WINDOWS):
            sl = slice(gi * DG_C, (gi + 1) * DG_C)
            acc = xx_scr[hist:hist + c, sl]
            for j in range(1, w):
                acc = acc + xx_scr[hist - j:hist - j + c, sl]
            cnt = jnp.minimum(w, pos1).astype(F32)
            pooled = acc / cnt - xc[:, sl]
            c_ref[rs, sl] = (_dot(pooled.astype(BF16), wp_ref[gi]) * ps_ref[:, sl]).astype(BF16)
        xx_scr[0:hist, :] = xx_scr[c:c + hist, :]

    @pl.when(n == pl.num_programs(1) - 1)
    def _():
        s_out_ref[...] = s_scr[...]
        buf_out_ref[...] = xx_scr[hist - POOL_BUF:hist, :]


def _mix_prompt(z, batch, seq, tabs, lw, layer):
    c = CHUNK
    rows = PROMPT_CHUNKS_PER_STEP * c
    nc = seq // rows
    t = z.shape[0]

    def zspec(width, blk):
        return pl.BlockSpec((rows, width), lambda b, n: (b * nc + n, blk))

    def whole(shape):
        nd = len(shape)
        return pl.BlockSpec((None,) + shape, lambda b, n: (layer,) + (0,) * nd)

    def const(shape):
        nd = len(shape)
        return pl.BlockSpec(shape, lambda b, n: (0,) * nd)

    in_specs = [
        zspec(D_A, BLK_U), zspec(D_A, BLK_V), zspec(QK_W, BLK_Q), zspec(QK_W, BLK_K),
        zspec(V_W, BLK_VR), zspec(V_W, BLK_GR), zspec(D_C, BLK_XC),
        pl.BlockSpec((rows, DK_RET), lambda b, n: (n, 0)),
        pl.BlockSpec((rows, DK_RET), lambda b, n: (n, 0)),
        whole((1, D_A)), whole((1, D_A)), whole((G_A, c, c)), whole((c, D_A)),
        const((H_RET, c, c)), const((H_RET, c, 1)), const((H_RET, c, 1)), const((H_RET, 1, 1)),
        whole((len(POOL_WINDOWS), DG_C, DG_C)), whole((1, D_C)),
    ]
    out_specs = [
        pl.BlockSpec((rows, D_A), lambda b, n: (b * nc + n, 0)),
        pl.BlockSpec((rows, V_W), lambda b, n: (b * nc + n, 0)),
        pl.BlockSpec((rows, D_C), lambda b, n: (b * nc + n, 0)),
        pl.BlockSpec((None, H_RET, DK_RET, DV_RET), lambda b, n: (b, 0, 0, 0)),
        pl.BlockSpec((None, POOL_BUF, D_C), lambda b, n: (b, 0, 0)),
    ]
    out_shape = [
        jax.ShapeDtypeStruct((t, D_A), BF16),
        jax.ShapeDtypeStruct((t, V_W), BF16),
        jax.ShapeDtypeStruct((t, D_C), BF16),
        jax.ShapeDtypeStruct((batch, H_RET, DK_RET, DV_RET), F32),
        jax.ShapeDtypeStruct((batch, POOL_BUF, D_C), F32),
    ]
    return pl.pallas_call(
        _mix_prompt_body,
        grid=(batch, nc),
        in_specs=in_specs,
        out_specs=out_specs,
        out_shape=out_shape,
        scratch_shapes=[pltpu.VMEM((H_RET, DK_RET, DV_RET), F32),
                        pltpu.VMEM((c + 2 * SUBLANES, D_C), F32)],
        compiler_params=_params(("arbitrary", "arbitrary"), 32),
        name="mix_prompt",
    )(z, z, z, z, z, z, z, tabs["cos"], tabs["sin"],
      lw["ln_g"], lw["ln_b"], lw["w_spatial"], lw["bmap_p"],
      tabs["din"], tabs["dq"], tabs["dk"], tabs["dc"],
      lw["w_pool"], lw["pool_scale"])


SAMPLE_NB = 8


def _mix_sample_body(u_ref, v_ref, q_ref, k_ref, vr_ref, gr_ref, xc_ref, cos_ref, sin_ref,
                     lng_ref, lnb_ref, wmix_ref, bmap_ref, din_ref, dq_ref, dk_ref, dc_ref,
                     wp_ref, ps_ref, s_in_ref, buf_in_ref, *rest, seq, pos0, n_alias):
    a_ref, o_ref, c_ref, vn_ref, s_out_ref, buf_out_ref, xx_scr = rest[n_alias:]
    nb = SAMPLE_NB
    r = nb * seq
    hist = 2 * SUBLANES

    u = jax.nn.gelu(u_ref[...])
    v = _layer_norm(jax.nn.gelu(v_ref[...]), lng_ref[...], lnb_ref[...])
    vn_ref[...] = v
    v3 = v.reshape(nb, seq, D_A)
    t_idx = lax.broadcasted_iota(I32, (seq, D_A), 0)
    mixed = jnp.broadcast_to(bmap_ref[...][None], (nb, seq, D_A))
    for s in range(seq):
        w_s = jnp.where(t_idx >= s, wmix_ref[s], 0.0)
        mixed = mixed + w_s[None] * v3[:, s:s + 1, :]
    a_ref[...] = (u * mixed.reshape(r, D_A)).astype(BF16)

    cos2 = cos_ref[...]
    sin2 = sin_ref[...]
    rowb = lax.broadcasted_iota(I32, (r, DV_RET), 0) // seq
    for h in range(H_RET):
        qs = slice(h * DK_RET, (h + 1) * DK_RET)
        vs = slice(h * DV_RET, (h + 1) * DV_RET)
        qr = _rotary(q_ref[:, qs], cos2, sin2)
        kr = _rotary(k_ref[:, qs], cos2, sin2) * (DK_RET ** -0.5)
        qb = qr.astype(BF16)
        vh = vr_ref[:, vs]
        scores = _dot_nt(qb, kr.astype(BF16)) * din_ref[h]
        inner = _dot(scores.astype(BF16), vh.astype(BF16))
        kd = (kr * dk_ref[h]).astype(BF16)
        cross = jnp.zeros((r, DV_RET), F32)
        for b in range(nb):
            s_old = s_in_ref[b, h]
            cross = jnp.where(rowb == b, _dot(qb, s_old.astype(BF16)), cross)
            vm = jnp.where(rowb == b, vh, 0.0).astype(BF16)
            s_out_ref[b, h] = dc_ref[h] * s_old + _dot_tn(kd, vm)
        oh = inner + cross * dq_ref[h]
        oh = oh * lax.rsqrt(jnp.mean(oh * oh, axis=-1, keepdims=True) + EPS)
        o_ref[:, vs] = (jax.nn.silu(gr_ref[:, vs]) * oh).astype(BF16)

    xc = xc_ref[...]
    xx_scr[:, hist - POOL_BUF:hist, :] = buf_in_ref[...]
    xx_scr[:, hist:hist + seq, :] = xc.reshape(nb, seq, D_C)
    pos1 = pos0 + 1 + lax.broadcasted_iota(I32, (seq, DG_C), 0)
    for gi, w in enumerate(POOL_WINDOWS):
        sl = slice(gi * DG_C, (gi + 1) * DG_C)
        acc = xx_scr[:, hist:hist + seq, sl]
        for j in range(1, w):
            acc = acc + xx_scr[:, hist - j:hist - j + seq, sl]
        cnt = jnp.minimum(w, pos1).astype(F32)
        pooled = (acc / cnt[None]).reshape(r, DG_C) - xc[:, sl]
        c_ref[:, sl] = (_dot(pooled.astype(BF16), wp_ref[gi]) * ps_ref[:, sl]).astype(BF16)
    buf_out_ref[...] = xx_scr[:, hist + seq - POOL_BUF:hist + seq, :]


def _mix_sample(z, a_buf, o_buf, c_buf, ret_buf, row0, batch, seq, pos0, tabs, lw,
                state_ret, state_pool, layer):
    nb = SAMPLE_NB
    r = nb * seq
    blk0 = row0 // r
    rows = batch * seq
    t = z.shape[0]

    def zspec(width, blk):
        return pl.BlockSpec((r, width), lambda i: (blk0 + i, blk))

    def whole(shape):
        nd = len(shape)
        return pl.BlockSpec((None,) + shape, lambda i: (layer,) + (0,) * nd)

    def const(shape):
        nd = len(shape)
        return pl.BlockSpec(shape, lambda i: (0,) * nd)

    in_specs = [
        zspec(D_A, BLK_U), zspec(D_A, BLK_V), zspec(QK_W, BLK_Q), zspec(QK_W, BLK_K),
        zspec(V_W, BLK_VR), zspec(V_W, BLK_GR), zspec(D_C, BLK_XC),
        const((r, DK_RET)), const((r, DK_RET)),
        whole((1, D_A)), whole((1, D_A)), whole((seq, seq, D_A)), whole((seq, D_A)),
        const((H_RET, r, r)), const((H_RET, r, 1)), const((H_RET, r, 1)), const((H_RET, 1, 1)),
        whole((len(POOL_WINDOWS), DG_C, DG_C)), whole((1, D_C)),
        pl.BlockSpec((None, nb, H_RET, DK_RET, DV_RET), lambda i: (layer, i, 0, 0, 0)),
        pl.BlockSpec((None, nb, POOL_BUF, D_C), lambda i: (layer, i, 0, 0)),
    ]
    args = [z, z, z, z, z, z, z, tabs["cos"], tabs["sin"],
            lw["ln_g"], lw["ln_b"], lw["wmix_s"], lw["bmap_s"],
            tabs["din"], tabs["dq"], tabs["dk"], tabs["dc"],
            lw["w_pool"], lw["pool_scale"], state_ret, state_pool]
    n_in = len(args)
    aliased = [a_buf, o_buf, c_buf] + ([ret_buf] if ret_buf is not None else [])
    aliases = {n_in + 0: 0, n_in + 1: 1, n_in + 2: 2}
    if ret_buf is not None:
        aliases[n_in + 3] = 4
    in_specs += [_any_spec()] * len(aliased)
    out_specs = [
        pl.BlockSpec((r, D_A), lambda i: (blk0 + i, 0)),
        pl.BlockSpec((r, V_W), lambda i: (blk0 + i, 0)),
        pl.BlockSpec((r, D_C), lambda i: (blk0 + i, 0)),
        pl.BlockSpec((r, D_A), lambda i: (i, 0)),
        pl.BlockSpec((None, nb, H_RET, DK_RET, DV_RET), lambda i: (layer, i, 0, 0, 0)),
        pl.BlockSpec((nb, POOL_BUF, D_C), lambda i: (i, 0, 0)),
    ]
    out_shape = [
        jax.ShapeDtypeStruct((t, D_A), BF16),
        jax.ShapeDtypeStruct((t, V_W), BF16),
        jax.ShapeDtypeStruct((t, D_C), BF16),
        jax.ShapeDtypeStruct((rows, D_A), F32),
        jax.ShapeDtypeStruct((DEPTH, batch, H_RET, DK_RET, DV_RET), F32),
        jax.ShapeDtypeStruct((batch, POOL_BUF, D_C), F32),
    ]
    return pl.pallas_call(
        functools.partial(_mix_sample_body, seq=seq, pos0=pos0, n_alias=len(aliased)),
        grid=(batch // nb,),
        in_specs=in_specs,
        out_specs=out_specs,
        out_shape=out_shape,
        scratch_shapes=[pltpu.VMEM((nb, 2 * SUBLANES + seq, D_C), F32)],
        input_output_aliases=aliases,
        compiler_params=_params(("parallel",), 40),
        name="mix_sample",
    )(*args, *aliased)


def _merge_body(a_ref, o_ref, c_ref, ga_ref, gb_ref, gc_ref, h_ref,
                wa_ref, wb_ref, wc_ref, wo_ref, out_ref):
    def gate(g_ref):
        return jax.nn.sigmoid(g_ref[...].astype(F32))

    m = gate(ga_ref) * _dot(a_ref[...], wa_ref[...])
    m = m + gate(gb_ref) * _dot(o_ref[...], wb_ref[...])
    m = m + gate(gc_ref) * _dot(c_ref[...], wc_ref[...])
    out_ref[...] = h_ref[...] + _dot(m.astype(BF16), wo_ref[...])


def _merge(a, o, c, sg, h, lw, layer):
    t = h.shape[0]
    tm = 512

    def rows(width, blk=0):
        return pl.BlockSpec((tm, width), lambda i: (i, blk))

    def whole(shape):
        nd = len(shape)
        return pl.BlockSpec((None,) + shape, lambda i: (layer,) + (0,) * nd)

    return pl.pallas_call(
        _merge_body,
        grid=(t // tm,),
        in_specs=[rows(D_A), rows(V_W), rows(D_C),
                  rows(D_MODEL, 0), rows(D_MODEL, 1), rows(D_MODEL, 2),
                  rows(D_MODEL),
                  whole((D_A, D_MODEL)), whole((V_W, D_MODEL)), whole((D_C, D_MODEL)),
                  whole((D_MODEL, D_MODEL))],
        out_specs=pl.BlockSpec((tm, D_MODEL), lambda i: (i, 0)),
        out_shape=jax.ShapeDtypeStruct((t, D_MODEL), F32),
        compiler_params=_params(("parallel",), 48),
        name="merge",
    )(a, o, c, sg, sg, sg, h, lw["w_br_a"], lw["w_br_b"], lw["w_br_c"], lw["w_out"])


def _ffn_body(x_ref, g_ref, gn_ref, wg_ref, wu_ref, wd_ref, out_ref, xnext_ref, xn_ref, acc_ref):
    j = pl.program_id(1)

    @pl.when(j == 0)
    def _():
        xn_ref[...] = _rms(x_ref[...], g_ref[...]).astype(BF16)
        acc_ref[...] = jnp.zeros_like(acc_ref)

    xn = xn_ref[...]
    hid = jax.nn.silu(_dot(xn, wg_ref[...])) * _dot(xn, wu_ref[...])
    acc_ref[...] += _dot(hid.astype(BF16), wd_ref[...])

    @pl.when(j == pl.num_programs(1) - 1)
    def _():
        out = x_ref[...] + acc_ref[...]
        out_ref[...] = out
        xnext_ref[...] = _rms(out, gn_ref[...]).astype(BF16)


def _ffn_dense(h, g_all, g_next_all, wg, wu, wd, layer, j):
    t = h.shape[0]
    tm, tf = 512, 1408
    return pl.pallas_call(
        _ffn_body,
        grid=(t // tm, D_FF // tf),
        in_specs=[
            pl.BlockSpec((tm, D_MODEL), lambda i, f: (i, 0)),
            pl.BlockSpec((None, 1, D_MODEL), lambda i, f: (layer, 0, 0)),
            pl.BlockSpec((None, 1, D_MODEL), lambda i, f: (layer + 1, 0, 0)),
            pl.BlockSpec((None, D_MODEL, tf), lambda i, f: (j, 0, f)),
            pl.BlockSpec((None, D_MODEL, tf), lambda i, f: (j, 0, f)),
            pl.BlockSpec((None, tf, D_MODEL), lambda i, f: (j, f, 0)),
        ],
        out_specs=[pl.BlockSpec((tm, D_MODEL), lambda i, f: (i, 0)),
                   pl.BlockSpec((tm, D_MODEL), lambda i, f: (i, 0))],
        out_shape=[jax.ShapeDtypeStruct((t, D_MODEL), F32),
                   jax.ShapeDtypeStruct((t, D_MODEL), BF16)],
        scratch_shapes=[pltpu.VMEM((tm, D_MODEL), BF16), pltpu.VMEM((tm, D_MODEL), F32)],
        compiler_params=_params(("parallel", "arbitrary"), 48),
        name="ffn_dense",
    )(h, g_all, g_next_all, wg, wu, wd)


def _router_body(x_ref, g_ref, wr_ref, idx_ref, wgt_ref, cnt_ref, carry_scr):
    @pl.when(pl.program_id(0) == 0)
    def _():
        carry_scr[...] = jnp.zeros_like(carry_scr)

    tm = x_ref.shape[0]
    xn = _rms(x_ref[...], g_ref[...])
    logits = jnp.dot(xn, wr_ref[...], preferred_element_type=F32, precision=lax.Precision.HIGHEST)
    lane = lax.broadcasted_iota(I32, logits.shape, 1)
    neg = jnp.float32(-jnp.inf)
    logits = jnp.where(lane < N_EXPERTS, logits, neg)
    v1 = jnp.max(logits, axis=-1, keepdims=True)
    i1 = jnp.min(jnp.where(logits == v1, lane, LANES), axis=-1, keepdims=True)
    rest = jnp.where(lane == i1, neg, logits)
    v2 = jnp.max(rest, axis=-1, keepdims=True)
    i2 = jnp.min(jnp.where(rest == v2, lane, LANES), axis=-1, keepdims=True)
    e2 = jnp.exp(v2 - v1)
    den = 1.0 + e2
    w1 = 1.0 / den
    w2 = e2 / den

    hit1 = lane == i1
    hit2 = lane == i2
    cnt = jnp.where(hit1, 1.0, jnp.where(hit2, 1.0, 0.0))
    r_i = lax.broadcasted_iota(I32, (tm, tm), 0)
    c_i = lax.broadcasted_iota(I32, (tm, tm), 1)
    strictly_lower = jnp.where(c_i < r_i, 1.0, 0.0).astype(BF16)
    before = _dot(strictly_lower, cnt.astype(BF16)) + carry_scr[...]
    rank1 = jnp.sum(jnp.where(hit1, before, 0.0), axis=-1, keepdims=True).astype(I32)
    rank2 = jnp.sum(jnp.where(hit2, before, 0.0), axis=-1, keepdims=True).astype(I32)
    carry_scr[...] += jnp.sum(cnt, axis=0, keepdims=True)

    idx_ref[...] = jnp.where(lane == 0, i1, jnp.where(lane == 1, i2,
                             jnp.where(lane == 2, rank1, jnp.where(lane == 3, rank2, 0))))
    wgt_ref[...] = jnp.where(lane == 0, w1, jnp.where(lane == 1, w2, 0.0))
    cnt_ref[...] = carry_scr[...].astype(I32)


def _router(h, g_all, wr_all, layer, j):
    t = h.shape[0]
    tm = ROW_TILE
    return pl.pallas_call(
        _router_body,
        grid=(t // tm,),
        in_specs=[
            pl.BlockSpec((tm, D_MODEL), lambda i: (i, 0)),
            pl.BlockSpec((None, 1, D_MODEL), lambda i: (layer, 0, 0)),
            pl.BlockSpec((None, D_MODEL, LANES), lambda i: (j, 0, 0)),
        ],
        out_specs=[pl.BlockSpec((tm, LANES), lambda i: (i, 0)),
                   pl.BlockSpec((tm, LANES), lambda i: (i, 0)),
                   pl.BlockSpec((1, LANES), lambda i: (0, 0))],
        out_shape=[jax.ShapeDtypeStruct((t, LANES), I32),
                   jax.ShapeDtypeStruct((t, LANES), F32),
                   jax.ShapeDtypeStruct((1, LANES), I32)],
        scratch_shapes=[pltpu.VMEM((1, LANES), F32)],
        compiler_params=_params(("arbitrary",), 32),
        name="router",
    )(h, g_all, wr_all)


def _routing_plan(idx, cnt, n_tiles):
    e1, e2, r1, r2 = idx[:, 0], idx[:, 1], idx[:, 2], idx[:, 3]
    counts = cnt[0, :N_EXPERTS]
    gsz = ((counts + MOE_TILE - 1) // MOE_TILE) * MOE_TILE
    gend = jnp.cumsum(gsz)
    gstart = gend - gsz
    pos1 = gstart[e1] + r1
    pos2 = gstart[e2] + r2
    n_used = gend[-1] // MOE_TILE
    tile_ids = jnp.arange(n_tiles, dtype=I32)
    texp = jnp.sum((tile_ids * MOE_TILE)[:, None] >= gend[None, :], axis=1).astype(I32)
    texp = jnp.minimum(texp, N_EXPERTS - 1)
    rows_in_tile = jnp.clip(counts[texp] - (tile_ids * MOE_TILE - gstart[texp]), 0, MOE_TILE)
    nsub = jnp.where(tile_ids < n_used, (rows_in_tile + MOE_SUB - 1) // MOE_SUB, 0)
    texp = jnp.where(tile_ids < n_used, texp, texp[jnp.maximum(n_used - 1, 0)])
    zrow = jnp.where(counts > 0, gstart + ((counts + MOE_SUB - 1) // MOE_SUB - 1) * MOE_SUB, -1)
    return (pos1.astype(I32), pos2.astype(I32), texp, nsub.astype(I32),
            n_used.astype(I32).reshape(1), zrow.astype(I32))


TOKEN_ROWS = D_MODEL // LANES
assert TOKEN_ROWS == SUBLANES


def _token_tile(ref, token):
    return ref.at[pl.ds(pl.multiple_of(token * TOKEN_ROWS, TOKEN_ROWS), TOKEN_ROWS)]


def _store_token_major(dst_ref, x):
    n = x.shape[0]
    for c in range(TOKEN_ROWS):
        dst_ref[pl.ds(c, n, stride=TOKEN_ROWS), :] = x[:, c * LANES:(c + 1) * LANES]


def _load_token_major(src_ref, n):
    return jnp.concatenate([src_ref[pl.ds(c, n, stride=TOKEN_ROWS), :] for c in range(TOKEN_ROWS)],
                           axis=1)


def _dispatch_body(pos1_ref, pos2_ref, zrow_ref, h_ref, g_ref, xs_ref, hn_scr, sems):
    tm = h_ref.shape[0]
    i = pl.program_id(0)
    base = i * tm
    step_rows = tm * TOKEN_ROWS
    slot = i % 2
    buf = hn_scr.at[slot]

    @pl.when(i == 0)
    def _():
        zeros = hn_scr.at[1]
        zeros[...] = jnp.zeros(zeros.shape, F32)
        for e in range(N_EXPERTS):
            zrow = zrow_ref[e]

            @pl.when(zrow >= 0)
            def _():
                for part in range(MOE_SUB // tm):
                    row0 = pl.multiple_of((zrow + part * tm) * TOKEN_ROWS, TOKEN_ROWS)
                    cp = pltpu.make_async_copy(zeros, xs_ref.at[pl.ds(row0, step_rows)], sems.at[1, 0])
                    cp.start()
                    cp.wait()

    _store_token_major(buf, _rms(h_ref[...], g_ref[...]))

    def start(t, carry):
        src = _token_tile(buf, t)
        for k, pos_ref in enumerate((pos1_ref, pos2_ref)):
            pltpu.make_async_copy(src, _token_tile(xs_ref, pos_ref[base + t]),
                                  sems.at[slot, k]).start(priority=k)
        return carry

    lax.fori_loop(0, tm, start, 0, unroll=8)

    def drain(s):
        for k in range(TOP_K):
            pltpu.make_async_copy(hn_scr.at[s], xs_ref.at[pl.ds(0, step_rows)], sems.at[s, k]).wait()

    @pl.when(i > 0)
    def _():
        drain(1 - slot)

    @pl.when(i == pl.num_programs(0) - 1)
    def _():
        drain(slot)


def _dispatch(h, g_all, pos1, pos2, zrow, n_rows, layer):
    t = h.shape[0]
    tm = ROW_TILE
    return pl.pallas_call(
        _dispatch_body,
        grid_spec=pltpu.PrefetchScalarGridSpec(
            num_scalar_prefetch=3,
            grid=(t // tm,),
            in_specs=[pl.BlockSpec((tm, D_MODEL), lambda i, *_: (i, 0)),
                      pl.BlockSpec((None, 1, D_MODEL), lambda i, *_: (layer, 0, 0))],
            out_specs=_any_spec(),
            scratch_shapes=[pltpu.VMEM((2, tm * TOKEN_ROWS, LANES), F32),
                            pltpu.SemaphoreType.DMA((2, TOP_K))],
        ),
        out_shape=jax.ShapeDtypeStruct((n_rows * TOKEN_ROWS, LANES), F32),
        compiler_params=_params(("arbitrary",), 32),
        name="moe_dispatch",
    )(pos1, pos2, zrow, h, g_all)


def _moe_group_body(texp_ref, nsub_ref, nused_ref, x_ref, wg_ref, wu_ref, wd_ref, y_ref,
                    xb_ref, acc_ref, wgb_ref, wub_ref, wdb_ref):
    i = pl.program_id(0)
    f = pl.program_id(1)
    sub = MOE_SUB

    @pl.when(i < nused_ref[0])
    def _():
        nsub = nsub_ref[i]
        wgb_ref[...] = wg_ref[...].astype(BF16)
        wub_ref[...] = wu_ref[...].astype(BF16)
        wdb_ref[...] = wd_ref[...].astype(BF16)

        for sb in range(MOE_TILE // sub):
            @pl.when(sb < nsub)
            def _():
                rows = pl.ds(sb * sub, sub)
                x_rows = x_ref.at[pl.ds(sb * sub * TOKEN_ROWS, sub * TOKEN_ROWS)]
                y_rows = y_ref.at[pl.ds(sb * sub * TOKEN_ROWS, sub * TOKEN_ROWS)]

                @pl.when(f == 0)
                def _():
                    xb_ref[rows, :] = _load_token_major(x_rows, sub).astype(BF16)
                    acc_ref[rows, :] = jnp.zeros((sub, D_MODEL), F32)

                xb = xb_ref[rows, :]
                hid = jax.nn.silu(_dot(xb, wgb_ref[...])) * _dot(xb, wub_ref[...])
                acc_ref[rows, :] += _dot(hid.astype(BF16), wdb_ref[...])

                @pl.when(f == pl.num_programs(1) - 1)
                def _():
                    _store_token_major(y_rows, acc_ref[rows, :])


def _moe_group(xs, texp, nsub, n_used, wg, wu, wd, j):
    n_rows = xs.shape[0] // TOKEN_ROWS
    tm, tf = MOE_TILE, 256
    nf = D_FF_EXP // tf

    def row_blk(i, nu):
        return jnp.maximum(jnp.minimum(i, nu[0] - 1), 0)

    def f_blk(i, f, nu):
        return jnp.where(i < nu[0], f, nf - 1)

    return pl.pallas_call(
        _moe_group_body,
        grid_spec=pltpu.PrefetchScalarGridSpec(
            num_scalar_prefetch=3,
            grid=(n_rows // tm, nf),
            in_specs=[
                pl.BlockSpec((tm * TOKEN_ROWS, LANES), lambda i, f, te, ns, nu: (row_blk(i, nu), 0)),
                pl.BlockSpec((None, None, D_MODEL, tf), lambda i, f, te, ns, nu: (j, te[i], 0, f_blk(i, f, nu))),
                pl.BlockSpec((None, None, D_MODEL, tf), lambda i, f, te, ns, nu: (j, te[i], 0, f_blk(i, f, nu))),
                pl.BlockSpec((None, None, tf, D_MODEL), lambda i, f, te, ns, nu: (j, te[i], f_blk(i, f, nu), 0)),
            ],
            out_specs=pl.BlockSpec((tm * TOKEN_ROWS, LANES), lambda i, f, te, ns, nu: (row_blk(i, nu), 0)),
            scratch_shapes=[pltpu.VMEM((tm, D_MODEL), BF16), pltpu.VMEM((tm, D_MODEL), F32),
                            pltpu.VMEM((D_MODEL, tf), BF16), pltpu.VMEM((D_MODEL, tf), BF16),
                            pltpu.VMEM((tf, D_MODEL), BF16)],
        ),
        out_shape=jax.ShapeDtypeStruct((n_rows * TOKEN_ROWS, LANES), F32),
        compiler_params=_params(("arbitrary", "arbitrary"), 60),
        name="moe_group",
    )(texp, nsub, n_used, xs, wg, wu, wd)


def _combine_body(pos1_ref, pos2_ref, h_ref, wgt_ref, gn_ref, ys_ref, *rest, last):
    *out_refs, y_scr, sems = rest
    tm = h_ref.shape[0]
    i = pl.program_id(0)
    slot = i % 2

    def fetch(step, s):
        base = step * tm

        def start(t, carry):
            for k, pos_ref in enumerate((pos1_ref, pos2_ref)):
                pltpu.make_async_copy(_token_tile(ys_ref, pos_ref[base + t]),
                                      _token_tile(y_scr.at[s, k], t), sems.at[s, k]).start(priority=k)
            return carry

        lax.fori_loop(0, tm, start, 0, unroll=8)

    @pl.when(i == 0)
    def _():
        fetch(0, 0)

    @pl.when(i + 1 < pl.num_programs(0))
    def _():
        fetch(i + 1, 1 - slot)

    for k in range(TOP_K):
        pltpu.make_async_copy(ys_ref.at[pl.ds(0, tm * TOKEN_ROWS)], y_scr.at[slot, k],
                              sems.at[slot, k]).wait()
    w = wgt_ref[...]
    out = h_ref[...] + (w[:, 0:1] * _load_token_major(y_scr.at[slot, 0], tm)
                        + w[:, 1:2] * _load_token_major(y_scr.at[slot, 1], tm))
    if last:
        out_refs[0][...] = _rms(out, gn_ref[...])
    else:
        out_refs[0][...] = out
        out_refs[1][...] = _rms(out, gn_ref[...]).astype(BF16)


def _combine(h, wgt, g_next, ys, pos1, pos2, last):
    t = h.shape[0]
    tm = ROW_TILE
    rows = pl.BlockSpec((tm, D_MODEL), lambda i, *_: (i, 0))
    full = jax.ShapeDtypeStruct((t, D_MODEL), F32)
    return pl.pallas_call(
        functools.partial(_combine_body, last=last),
        grid_spec=pltpu.PrefetchScalarGridSpec(
            num_scalar_prefetch=2,
            grid=(t // tm,),
            in_specs=[rows,
                      pl.BlockSpec((tm, LANES), lambda i, *_: (i, 0)),
                      pl.BlockSpec((1, D_MODEL), lambda i, *_: (0, 0)),
                      _any_spec()],
            out_specs=[rows] if last else [rows, rows],
            scratch_shapes=[pltpu.VMEM((2, TOP_K, tm * TOKEN_ROWS, LANES), F32),
                            pltpu.SemaphoreType.DMA((2, TOP_K))],
        ),
        out_shape=[full] if last else [full, jax.ShapeDtypeStruct((t, D_MODEL), BF16)],
        compiler_params=_params(("arbitrary",), 32),
        name="moe_combine",
    )(pos1, pos2, h, wgt, g_next, ys)


def _ffn_moe(h, g_all, g_next, wr_all, wg, wu, wd, layer, j, last):
    t = h.shape[0]
    n_rows = TOP_K * t + N_EXPERTS * MOE_TILE
    idx, wgt, cnt = _router(h, g_all, wr_all, layer, j)
    pos1, pos2, texp, nsub, n_used, zrow = _routing_plan(idx, cnt, n_rows // MOE_TILE)
    xs = _dispatch(h, g_all, pos1, pos2, zrow, n_rows, layer)
    ys = _moe_group(xs, texp, nsub, n_used, wg, wu, wd, j)
    return _combine(h, wgt, g_next, ys, pos1, pos2, last)


def _rope_tables(pos):
    half = DK_RET // 2
    inv = ROPE_BASE ** (-jnp.arange(half, dtype=F32) / half)
    ang = pos.astype(F32)[:, None] * inv[None, :]
    cos, sin = jnp.cos(ang), jnp.sin(ang)
    return jnp.concatenate([cos, cos], axis=-1), jnp.concatenate([-sin, sin], axis=-1)


def _decay_tables(c):
    log_g = jnp.log1p(-jnp.exp2(-5.0 - jnp.arange(H_RET, dtype=F32)))
    idx = jnp.arange(c, dtype=F32)
    diff = idx[:, None] - idx[None, :]
    causal = diff >= 0
    d_in = jnp.where(causal[None], jnp.exp(log_g[:, None, None] * jnp.where(causal, diff, 0.0)[None]), 0.0)
    d_q = jnp.exp(log_g[:, None] * (idx[None, :] + 1.0))
    d_k = jnp.exp(log_g[:, None] * (c - 1.0 - idx[None, :]))
    d_c = jnp.exp(log_g * c)
    return d_in, d_q, d_k, d_c


def _prompt_tables(seq):
    c = min(CHUNK, seq)
    cos2, sin2 = _rope_tables(jnp.arange(seq, dtype=I32))
    d_in, d_q, d_k, d_c = _decay_tables(c)
    return dict(cos=cos2, sin=sin2, din=d_in, dq=d_q[:, :, None], dk=d_k[:, :, None],
                dc=d_c[:, None, None])


def _sample_tables(seq, pos0):
    nb = SAMPLE_NB
    cos2, sin2 = _rope_tables(pos0 + jnp.arange(seq, dtype=I32))
    d_in, d_q, d_k, d_c = _decay_tables(seq)
    eye = jnp.eye(nb, dtype=F32)
    d_blk = jnp.einsum("ab,hij->haibj", eye, d_in).reshape(H_RET, nb * seq, nb * seq)
    return dict(cos=jnp.tile(cos2, (nb, 1)), sin=jnp.tile(sin2, (nb, 1)), din=d_blk,
                dq=jnp.tile(d_q, (1, nb))[:, :, None], dk=jnp.tile(d_k, (1, nb))[:, :, None],
                dc=d_c[:, None, None])


def kernel(x_prompt, x_sample, state_ret, state_pool, norm1_g, w_in, ln_v_g, ln_v_b, w_spatial, b_spatial, w_pool, pool_scale, w_br_a, w_br_b, w_br_c, w_out, norm2_g, w_ffn_gate, w_ffn_up, w_ffn_down, w_router, w_exp_gate, w_exp_up, w_exp_down, final_norm_g):
    batch, seq, _ = x_prompt.shape
    dec_batch, dec_seq, _ = x_sample.shape
    n_prompt = batch * seq
    n_sample = dec_batch * dec_seq
    assert seq % (PROMPT_CHUNKS_PER_STEP * CHUNK) == 0 and dec_seq <= CHUNK and PAST_LEN % CHUNK == 0
    assert n_prompt % (SAMPLE_NB * dec_seq) == 0 and dec_batch % SAMPLE_NB == 0

    lw = dict(
        ln_g=ln_v_g[:, None, :], ln_b=ln_v_b[:, None, :],
        w_spatial=w_spatial,
        bmap_p=jnp.repeat(jnp.swapaxes(b_spatial, 1, 2), DG_A, axis=2),
        wmix_s=jnp.repeat(jnp.transpose(w_spatial[:, :, :dec_seq, :dec_seq], (0, 3, 2, 1)), DG_A, axis=3),
        w_pool=w_pool.astype(BF16), pool_scale=pool_scale[:, None, :],
        w_br_a=w_br_a.astype(BF16), w_br_b=w_br_b.astype(BF16), w_br_c=w_br_c.astype(BF16),
        w_out=w_out.astype(BF16),
    )
    lw["bmap_s"] = lw["bmap_p"][:, :dec_seq, :]
    g1 = norm1_g[:, None, :]
    g2 = norm2_g[:, None, :]
    wfg, wfu, wfd = w_ffn_gate.astype(BF16), w_ffn_up.astype(BF16), w_ffn_down.astype(BF16)
    wr = jnp.pad(w_router, ((0, 0), (0, 0), (0, LANES - N_EXPERTS)))
    tabs_p = _prompt_tables(seq)
    tabs_s = _sample_tables(dec_seq, PAST_LEN)

    h = jnp.concatenate([x_prompt.reshape(n_prompt, D_MODEL),
                         x_sample.reshape(n_sample, D_MODEL)], axis=0)
    ret_p, pool_p, pool_s, v_s = [], [], [], []
    ret_s = None
    assert DEPTH % 2 == 0
    xn = _norm_cast(h, g1, 0)
    for layer in range(DEPTH):
        j = layer // 2
        last = layer == DEPTH - 1
        z = _inproj(xn, w_in, layer, gates=False)
        sg = _inproj(xn, w_in, layer, gates=True)
        a, o, c, s_p, b_p = _mix_prompt(z, batch, seq, tabs_p, lw, layer)
        a, o, c, vn, ret_s, b_s = _mix_sample(z, a, o, c, ret_s, n_prompt, dec_batch, dec_seq, PAST_LEN,
                                              tabs_s, lw, state_ret, state_pool, layer)
        h = _merge(a, o, c, sg, h, lw, layer)
        if layer % 2 == 0:
            h, xn = _ffn_dense(h, g2, g1, wfg, wfu, wfd, layer, j)
        else:
            g_next = final_norm_g[None, :] if last else g1[layer + 1]
            res = _ffn_moe(h, g2, g_next, wr, w_exp_gate, w_exp_up, w_exp_down, layer, j, last)
            if last:
                y, = res
            else:
                h, xn = res
        ret_p.append(s_p)
        pool_p.append(b_p)
        pool_s.append(b_s)
        v_s.append(vn.reshape(dec_batch, dec_seq, D_A))

    y_prompt = y[:n_prompt].reshape(batch, seq, D_MODEL)
    y_sample = y[n_prompt:].reshape(dec_batch, dec_seq, D_MODEL)
    return (y_prompt, y_sample, jnp.stack(ret_p), ret_s, jnp.stack(pool_p),
            jnp.stack(pool_s), jnp.stack(v_s))
```

```python
import functools

import jax
import jax.numpy as jnp
from jax import lax
from jax.experimental import pallas as pl
from jax.experimental.pallas import tpu as pltpu

F32 = jnp.float32
BF16 = jnp.bfloat16
I32 = jnp.int32

D_MODEL = 1024
DEPTH = 4
PAST_LEN = 16384
D_A = D_MODEL // 2
CHUNK = 128
G_A = 4
DG_A = D_A // G_A
H_RET = 4
DK_RET = D_MODEL // 2 // H_RET
DV_RET = 2 * DK_RET
ROPE_BASE = 10000.0
D_C = D_MODEL // 2
POOL_WINDOWS = (2, 4, 8, 16)
DG_C = D_C // len(POOL_WINDOWS)
POOL_BUF = max(POOL_WINDOWS) - 1
D_FF = 2816
N_EXPERTS = 8
TOP_K = 2
D_FF_EXP = 3584
EPS = 1e-6
QK_W = H_RET * DK_RET
V_W = H_RET * DV_RET
IN_COLS = 2 * D_A + 2 * QK_W + 2 * V_W + D_C + 3 * D_MODEL

LANES = 128
SUBLANES = 8
MIB = 1024 * 1024

BLK_U, BLK_V, BLK_Q, BLK_K, BLK_XC = 0, 1, 2, 3, 8
BLK_VR, BLK_GR = 2, 3
MIX_COLS = 2 * D_A + 2 * QK_W + 2 * V_W + D_C

PROMPT_CHUNKS_PER_STEP = 4
MOE_TILE = 1024
ROW_TILE = 1024


def _params(sem, vmem_mib):
    return pltpu.CompilerParams(dimension_semantics=sem, vmem_limit_bytes=vmem_mib * MIB)


def _rms(x, g):
    return x * lax.rsqrt(jnp.mean(x * x, axis=-1, keepdims=True) + EPS) * g


def _layer_norm(x, g, b):
    mu = jnp.mean(x, axis=-1, keepdims=True)
    xc = x - mu
    var = jnp.mean(xc * xc, axis=-1, keepdims=True)
    return xc * lax.rsqrt(var + EPS) * g + b


def _dot(a, b):
    return jnp.dot(a, b, preferred_element_type=F32)


def _dot_nt(a, b):
    return lax.dot_general(a, b, (((1,), (1,)), ((), ())), preferred_element_type=F32)


def _dot_tn(a, b):
    return lax.dot_general(a, b, (((0,), (0,)), ((), ())), preferred_element_type=F32)


def _rotary(x, cos2, sin2):
    return x * cos2 + pltpu.roll(x, DK_RET // 2, 1) * sin2


def _any_spec():
    return pl.BlockSpec(memory_space=pl.ANY)


def _norm_cast_body(x_ref, g_ref, o_ref):
    o_ref[...] = _rms(x_ref[...], g_ref[...]).astype(BF16)


def _norm_cast(h, g_all, layer):
    t = h.shape[0]
    tm = 1024
    return pl.pallas_call(
        _norm_cast_body,
        grid=(t // tm,),
        in_specs=[pl.BlockSpec((tm, D_MODEL), lambda i: (i, 0)),
                  pl.BlockSpec((None, 1, D_MODEL), lambda i: (layer, 0, 0))],
        out_specs=pl.BlockSpec((tm, D_MODEL), lambda i: (i, 0)),
        out_shape=jax.ShapeDtypeStruct((t, D_MODEL), BF16),
        compiler_params=_params(("parallel",), 32),
        name="norm_cast",
    )(h, g_all)


def _inproj_body(x_ref, w_ref, o_ref, wb_ref):
    @pl.when(pl.program_id(1) == 0)
    def _():
        wb_ref[...] = w_ref[...].astype(BF16)

    o_ref[...] = _dot(x_ref[...], wb_ref[...]).astype(o_ref.dtype)


def _inproj(xn, w_all, layer, gates):
    t = xn.shape[0]
    tm, tn = 1024, 1536
    col0, cols, dtype = (MIX_COLS // tn, IN_COLS - MIX_COLS, BF16) if gates else (0, MIX_COLS, F32)
    return pl.pallas_call(
        _inproj_body,
        grid=(cols // tn, t // tm),
        in_specs=[
            pl.BlockSpec((tm, D_MODEL), lambda j, i: (i, 0)),
            pl.BlockSpec((None, D_MODEL, tn), lambda j, i: (layer, 0, col0 + j)),
        ],
        out_specs=pl.BlockSpec((tm, tn), lambda j, i: (i, j)),
        out_shape=jax.ShapeDtypeStruct((t, cols), dtype),
        scratch_shapes=[pltpu.VMEM((D_MODEL, tn), BF16)],
        compiler_params=_params(("parallel", "arbitrary"), 48),
        name="inproj_gates" if gates else "inproj",
    )(xn, w_all)


def _mix_prompt_body(u_ref, v_ref, q_ref, k_ref, vr_ref, gr_ref, xc_ref, cos_ref, sin_ref,
                     lng_ref, lnb_ref, ws_ref, bmap_ref, din_ref, dq_ref, dk_ref, dc_ref,
                     wp_ref, ps_ref,
                     a_ref, o_ref, c_ref, s_out_ref, buf_out_ref,
                     s_scr, xx_scr):
    n = pl.program_id(1)
    c = CHUNK
    hist = 2 * SUBLANES

    @pl.when(n == 0)
    def _():
        s_scr[...] = jnp.zeros_like(s_scr)
        xx_scr[0:hist, :] = jnp.zeros((hist, D_C), F32)

    row = lax.broadcasted_iota(I32, (c, c), 0)
    col = lax.broadcasted_iota(I32, (c, c), 1)

    for sub in range(PROMPT_CHUNKS_PER_STEP):
        rs = slice(sub * c, (sub + 1) * c)

        u = jax.nn.gelu(u_ref[rs, :])
        v = _layer_norm(jax.nn.gelu(v_ref[rs, :]), lng_ref[...], lnb_ref[...])
        for g in range(G_A):
            sl = slice(g * DG_A, (g + 1) * DG_A)
            w = jnp.where(col <= row, ws_ref[g], 0.0).astype(BF16)
            mixed = _dot(w, v[:, sl].astype(BF16)) + bmap_ref[:, sl]
            a_ref[rs, sl] = (u[:, sl] * mixed).astype(BF16)

        cos2 = cos_ref[rs, :]
        sin2 = sin_ref[rs, :]
        for h in range(H_RET):
            qs = slice(h * DK_RET, (h + 1) * DK_RET)
            vs = slice(h * DV_RET, (h + 1) * DV_RET)
            qr = _rotary(q_ref[rs, qs], cos2, sin2)
            kr = _rotary(k_ref[rs, qs], cos2, sin2) * (DK_RET ** -0.5)
            qb = qr.astype(BF16)
            vb = vr_ref[rs, vs].astype(BF16)
            scores = _dot_nt(qb, kr.astype(BF16)) * din_ref[h]
            inner = _dot(scores.astype(BF16), vb)
            s_old = s_scr[h]
            cross = _dot(qb, s_old.astype(BF16)) * dq_ref[h]
            kd = (kr * dk_ref[h]).astype(BF16)
            s_scr[h] = dc_ref[h] * s_old + _dot_tn(kd, vb)
            oh = inner + cross
            oh = oh * lax.rsqrt(jnp.mean(oh * oh, axis=-1, keepdims=True) + EPS)
            o_ref[rs, vs] = (jax.nn.silu(gr_ref[rs, vs]) * oh).astype(BF16)

        xc = xc_ref[rs, :]
        xx_scr[hist:hist + c, :] = xc
        pos1 = (n * PROMPT_CHUNKS_PER_STEP + sub) * c + 1 + lax.broadcasted_iota(I32, (c, DG_C), 0)
        for gi, w in enumerate(POOL_WINDOWS):
            sl = slice(gi * DG_C, (gi + 1) * DG_C)
            acc = xx_scr[hist:hist + c, sl]
            for j in range(1, w):
                acc = acc + xx_scr[hist - j:hist - j + c, sl]
            cnt = jnp.minimum(w, pos1).astype(F32)
            pooled = acc / cnt - xc[:, sl]
            c_ref[rs, sl] = (_dot(pooled.astype(BF16), wp_ref[gi]) * ps_ref[:, sl]).astype(BF16)
        xx_scr[0:hist, :] = xx_scr[c:c + hist, :]

    @pl.when(n == pl.num_programs(1) - 1)
    def _():
        s_out_ref[...] = s_scr[...]
        buf_out_ref[...] = xx_scr[hist - POOL_BUF:hist, :]


def _mix_prompt(z, batch, seq, tabs, lw, layer):
    c = CHUNK
    rows = PROMPT_CHUNKS_PER_STEP * c
    nc = seq // rows
    t = z.shape[0]

    def zspec(width, blk):
        return pl.BlockSpec((rows, width), lambda b, n: (b * nc + n, blk))

    def whole(shape):
        nd = len(shape)
        return pl.BlockSpec((None,) + shape, lambda b, n: (layer,) + (0,) * nd)

    def const(shape):
        nd = len(shape)
        return pl.BlockSpec(shape, lambda b, n: (0,) * nd)

    in_specs = [
        zspec(D_A, BLK_U), zspec(D_A, BLK_V), zspec(QK_W, BLK_Q), zspec(QK_W, BLK_K),
        zspec(V_W, BLK_VR), zspec(V_W, BLK_GR), zspec(D_C, BLK_XC),
        pl.BlockSpec((rows, DK_RET), lambda b, n: (n, 0)),
        pl.BlockSpec((rows, DK_RET), lambda b, n: (n, 0)),
        whole((1, D_A)), whole((1, D_A)), whole((G_A, c, c)), whole((c, D_A)),
        const((H_RET, c, c)), const((H_RET, c, 1)), const((H_RET, c, 1)), const((H_RET, 1, 1)),
        whole((len(POOL_WINDOWS), DG_C, DG_C)), whole((1, D_C)),
    ]
    out_specs = [
        pl.BlockSpec((rows, D_A), lambda b, n: (b * nc + n, 0)),
        pl.BlockSpec((rows, V_W), lambda b, n: (b * nc + n, 0)),
        pl.BlockSpec((rows, D_C), lambda b, n: (b * nc + n, 0)),
        pl.BlockSpec((None, H_RET, DK_RET, DV_RET), lambda b, n: (b, 0, 0, 0)),
        pl.BlockSpec((None, POOL_BUF, D_C), lambda b, n: (b, 0, 0)),
    ]
    out_shape = [
        jax.ShapeDtypeStruct((t, D_A), BF16),
        jax.ShapeDtypeStruct((t, V_W), BF16),
        jax.ShapeDtypeStruct((t, D_C), BF16),
        jax.ShapeDtypeStruct((batch, H_RET, DK_RET, DV_RET), F32),
        jax.ShapeDtypeStruct((batch, POOL_BUF, D_C), F32),
    ]
    return pl.pallas_call(
        _mix_prompt_body,
        grid=(batch, nc),
        in_specs=in_specs,
        out_specs=out_specs,
        out_shape=out_shape,
        scratch_shapes=[pltpu.VMEM((H_RET, DK_RET, DV_RET), F32),
                        pltpu.VMEM((c + 2 * SUBLANES, D_C), F32)],
        compiler_params=_params(("arbitrary", "arbitrary"), 32),
        name="mix_prompt",
    )(z, z, z, z, z, z, z, tabs["cos"], tabs["sin"],
      lw["ln_g"], lw["ln_b"], lw["w_spatial"], lw["bmap_p"],
      tabs["din"], tabs["dq"], tabs["dk"], tabs["dc"],
      lw["w_pool"], lw["pool_scale"])


SAMPLE_NB = 8


def _mix_sample_body(u_ref, v_ref, q_ref, k_ref, vr_ref, gr_ref, xc_ref, cos_ref, sin_ref,
                     lng_ref, lnb_ref, wmix_ref, bmap_ref, din_ref, dq_ref, dk_ref, dc_ref,
                     wp_ref, ps_ref, s_in_ref, buf_in_ref, *rest, seq, pos0, n_alias):
    a_ref, o_ref, c_ref, vn_ref, s_out_ref, buf_out_ref, xx_scr = rest[n_alias:]
    nb = SAMPLE_NB
    r = nb * seq
    hist = 2 * SUBLANES

    u = jax.nn.gelu(u_ref[...])
    v = _layer_norm(jax.nn.gelu(v_ref[...]), lng_ref[...], lnb_ref[...])
    vn_ref[...] = v
    v3 = v.reshape(nb, seq, D_A)
    t_idx = lax.broadcasted_iota(I32, (seq, D_A), 0)
    mixed = jnp.broadcast_to(bmap_ref[...][None], (nb, seq, D_A))
    for s in range(seq):
        w_s = jnp.where(t_idx >= s, wmix_ref[s], 0.0)
        mixed = mixed + w_s[None] * v3[:, s:s + 1, :]
    a_ref[...] = (u * mixed.reshape(r, D_A)).astype(BF16)

    cos2 = cos_ref[...]
    sin2 = sin_ref[...]
    rowb = lax.broadcasted_iota(I32, (r, DV_RET), 0) // seq
    for h in range(H_RET):
        qs = slice(h * DK_RET, (h + 1) * DK_RET)
        vs = slice(h * DV_RET, (h + 1) * DV_RET)
        qr = _rotary(q_ref[:, qs], cos2, sin2)
        kr = _rotary(k_ref[:, qs], cos2, sin2) * (DK_RET ** -0.5)
        qb = qr.astype(BF16)
        vh = vr_ref[:, vs]
        scores = _dot_nt(qb, kr.astype(BF16)) * din_ref[h]
        inner = _dot(scores.astype(BF16), vh.astype(BF16))
        kd = (kr * dk_ref[h]).astype(BF16)
        cross = jnp.zeros((r, DV_RET), F32)
        for b in range(nb):
            s_old = s_in_ref[b, h]
            cross = jnp.where(rowb == b, _dot(qb, s_old.astype(BF16)), cross)
            vm = jnp.where(rowb == b, vh, 0.0).astype(BF16)
            s_out_ref[b, h] = dc_ref[h] * s_old + _dot_tn(kd, vm)
        oh = inner + cross * dq_ref[h]
        oh = oh * lax.rsqrt(jnp.mean(oh * oh, axis=-1, keepdims=True) + EPS)
        o_ref[:, vs] = (jax.nn.silu(gr_ref[:, vs]) * oh).astype(BF16)

    xc = xc_ref[...]
    xx_scr[:, hist - POOL_BUF:hist, :] = buf_in_ref[...]
    xx_scr[:, hist:hist + seq, :] = xc.reshape(nb, seq, D_C)
    pos1 = pos0 + 1 + lax.broadcasted_iota(I32, (seq, DG_C), 0)
    for gi, w in enumerate(POOL_WINDOWS):
        sl = slice(gi * DG_C, (gi + 1) * DG_C)
        acc = xx_scr[:, hist:hist + seq, sl]
        for j in range(1, w):
            acc = acc + xx_scr[:, hist - j:hist - j + seq, sl]
        cnt = jnp.minimum(w, pos1).astype(F32)
        pooled = (acc / cnt[None]).reshape(r, DG_C) - xc[:, sl]
        c_ref[:, sl] = (_dot(pooled.astype(BF16), wp_ref[gi]) * ps_ref[:, sl]).astype(BF16)
    buf_out_ref[...] = xx_scr[:, hist + seq - POOL_BUF:hist + seq, :]


def _mix_sample(z, a_buf, o_buf, c_buf, ret_buf, row0, batch, seq, pos0, tabs, lw,
                state_ret, state_pool, layer):
    nb = SAMPLE_NB
    r = nb * seq
    blk0 = row0 // r
    rows = batch * seq
    t = z.shape[0]

    def zspec(width, blk):
        return pl.BlockSpec((r, width), lambda i: (blk0 + i, blk))

    def whole(shape):
        nd = len(shape)
        return pl.BlockSpec((None,) + shape, lambda i: (layer,) + (0,) * nd)

    def const(shape):
        nd = len(shape)
        return pl.BlockSpec(shape, lambda i: (0,) * nd)

    in_specs = [
        zspec(D_A, BLK_U), zspec(D_A, BLK_V), zspec(QK_W, BLK_Q), zspec(QK_W, BLK_K),
        zspec(V_W, BLK_VR), zspec(V_W, BLK_GR), zspec(D_C, BLK_XC),
        const((r, DK_RET)), const((r, DK_RET)),
        whole((1, D_A)), whole((1, D_A)), whole((seq, seq, D_A)), whole((seq, D_A)),
        const((H_RET, r, r)), const((H_RET, r, 1)), const((H_RET, r, 1)), const((H_RET, 1, 1)),
        whole((len(POOL_WINDOWS), DG_C, DG_C)), whole((1, D_C)),
        pl.BlockSpec((None, nb, H_RET, DK_RET, DV_RET), lambda i: (layer, i, 0, 0, 0)),
        pl.BlockSpec((None, nb, POOL_BUF, D_C), lambda i: (layer, i, 0, 0)),
    ]
    args = [z, z, z, z, z, z, z, tabs["cos"], tabs["sin"],
            lw["ln_g"], lw["ln_b"], lw["wmix_s"], lw["bmap_s"],
            tabs["din"], tabs["dq"], tabs["dk"], tabs["dc"],
            lw["w_pool"], lw["pool_scale"], state_ret, state_pool]
    n_in = len(args)
    aliased = [a_buf, o_buf, c_buf] + ([ret_buf] if ret_buf is not None else [])
    aliases = {n_in + 0: 0, n_in + 1: 1, n_in + 2: 2}
    if ret_buf is not None:
        aliases[n_in + 3] = 4
    in_specs += [_any_spec()] * len(aliased)
    out_specs = [
        pl.BlockSpec((r, D_A), lambda i: (blk0 + i, 0)),
        pl.BlockSpec((r, V_W), lambda i: (blk0 + i, 0)),
        pl.BlockSpec((r, D_C), lambda i: (blk0 + i, 0)),
        pl.BlockSpec((r, D_A), lambda i: (i, 0)),
        pl.BlockSpec((None, nb, H_RET, DK_RET, DV_RET), lambda i: (layer, i, 0, 0, 0)),
        pl.BlockSpec((nb, POOL_BUF, D_C), lambda i: (i, 0, 0)),
    ]
    out_shape = [
        jax.ShapeDtypeStruct((t, D_A), BF16),
        jax.ShapeDtypeStruct((t, V_W), BF16),
        jax.ShapeDtypeStruct((t, D_C), BF16),
        jax.ShapeDtypeStruct((rows, D_A), F32),
        jax.ShapeDtypeStruct((DEPTH, batch, H_RET, DK_RET, DV_RET), F32),
        jax.ShapeDtypeStruct((batch, POOL_BUF, D_C), F32),
    ]
    return pl.pallas_call(
        functools.partial(_mix_sample_body, seq=seq, pos0=pos0, n_alias=len(aliased)),
        grid=(batch // nb,),
        in_specs=in_specs,
        out_specs=out_specs,
        out_shape=out_shape,
        scratch_shapes=[pltpu.VMEM((nb, 2 * SUBLANES + seq, D_C), F32)],
        input_output_aliases=aliases,
        compiler_params=_params(("parallel",), 40),
        name="mix_sample",
    )(*args, *aliased)


def _merge_body(a_ref, o_ref, c_ref, ga_ref, gb_ref, gc_ref, h_ref,
                wa_ref, wb_ref, wc_ref, wo_ref, out_ref):
    def gate(g_ref):
        return jax.nn.sigmoid(g_ref[...].astype(F32))

    m = gate(ga_ref) * _dot(a_ref[...], wa_ref[...])
    m = m + gate(gb_ref) * _dot(o_ref[...], wb_ref[...])
    m = m + gate(gc_ref) * _dot(c_ref[...], wc_ref[...])
    out_ref[...] = h_ref[...] + _dot(m.astype(BF16), wo_ref[...])


def _merge(a, o, c, sg, h, lw, layer):
    t = h.shape[0]
    tm = 512

    def rows(width, blk=0):
        return pl.BlockSpec((tm, width), lambda i: (i, blk))

    def whole(shape):
        nd = len(shape)
        return pl.BlockSpec((None,) + shape, lambda i: (layer,) + (0,) * nd)

    return pl.pallas_call(
        _merge_body,
        grid=(t // tm,),
        in_specs=[rows(D_A), rows(V_W), rows(D_C),
                  rows(D_MODEL, 0), rows(D_MODEL, 1), rows(D_MODEL, 2),
                  rows(D_MODEL),
                  whole((D_A, D_MODEL)), whole((V_W, D_MODEL)), whole((D_C, D_MODEL)),
                  whole((D_MODEL, D_MODEL))],
        out_specs=pl.BlockSpec((tm, D_MODEL), lambda i: (i, 0)),
        out_shape=jax.ShapeDtypeStruct((t, D_MODEL), F32),
        compiler_params=_params(("parallel",), 48),
        name="merge",
    )(a, o, c, sg, sg, sg, h, lw["w_br_a"], lw["w_br_b"], lw["w_br_c"], lw["w_out"])


def _ffn_body(x_ref, g_ref, gn_ref, wg_ref, wu_ref, wd_ref, out_ref, xnext_ref, xn_ref, acc_ref):
    j = pl.program_id(1)

    @pl.when(j == 0)
    def _():
        xn_ref[...] = _rms(x_ref[...], g_ref[...]).astype(BF16)
        acc_ref[...] = jnp.zeros_like(acc_ref)

    xn = xn_ref[...]
    hid = jax.nn.silu(_dot(xn, wg_ref[...])) * _dot(xn, wu_ref[...])
    acc_ref[...] += _dot(hid.astype(BF16), wd_ref[...])

    @pl.when(j == pl.num_programs(1) - 1)
    def _():
        out = x_ref[...] + acc_ref[...]
        out_ref[...] = out
        xnext_ref[...] = _rms(out, gn_ref[...]).astype(BF16)


def _ffn_dense(h, g_all, g_next_all, wg, wu, wd, layer, j):
    t = h.shape[0]
    tm, tf = 512, 1408
    return pl.pallas_call(
        _ffn_body,
        grid=(t // tm, D_FF // tf),
        in_specs=[
            pl.BlockSpec((tm, D_MODEL), lambda i, f: (i, 0)),
            pl.BlockSpec((None, 1, D_MODEL), lambda i, f: (layer, 0, 0)),
            pl.BlockSpec((None, 1, D_MODEL), lambda i, f: (layer + 1, 0, 0)),
            pl.BlockSpec((None, D_MODEL, tf), lambda i, f: (j, 0, f)),
            pl.BlockSpec((None, D_MODEL, tf), lambda i, f: (j, 0, f)),
            pl.BlockSpec((None, tf, D_MODEL), lambda i, f: (j, f, 0)),
        ],
        out_specs=[pl.BlockSpec((tm, D_MODEL), lambda i, f: (i, 0)),
                   pl.BlockSpec((tm, D_MODEL), lambda i, f: (i, 0))],
        out_shape=[jax.ShapeDtypeStruct((t, D_MODEL), F32),
                   jax.ShapeDtypeStruct((t, D_MODEL), BF16)],
        scratch_shapes=[pltpu.VMEM((tm, D_MODEL), BF16), pltpu.VMEM((tm, D_MODEL), F32)],
        compiler_params=_params(("parallel", "arbitrary"), 48),
        name="ffn_dense",
    )(h, g_all, g_next_all, wg, wu, wd)


def _router_body(x_ref, g_ref, wr_ref, idx_ref, wgt_ref, cnt_ref, carry_scr):
    @pl.when(pl.program_id(0) == 0)
    def _():
        carry_scr[...] = jnp.zeros_like(carry_scr)

    tm = x_ref.shape[0]
    xn = _rms(x_ref[...], g_ref[...])
    logits = jnp.dot(xn, wr_ref[...], preferred_element_type=F32, precision=lax.Precision.HIGHEST)
    lane = lax.broadcasted_iota(I32, logits.shape, 1)
    neg = jnp.float32(-jnp.inf)
    logits = jnp.where(lane < N_EXPERTS, logits, neg)
    v1 = jnp.max(logits, axis=-1, keepdims=True)
    i1 = jnp.min(jnp.where(logits == v1, lane, LANES), axis=-1, keepdims=True)
    rest = jnp.where(lane == i1, neg, logits)
    v2 = jnp.max(rest, axis=-1, keepdims=True)
    i2 = jnp.min(jnp.where(rest == v2, lane, LANES), axis=-1, keepdims=True)
    e2 = jnp.exp(v2 - v1)
    den = 1.0 + e2
    w1 = 1.0 / den
    w2 = e2 / den

    hit1 = lane == i1
    hit2 = lane == i2
    cnt = jnp.where(hit1, 1.0, jnp.where(hit2, 1.0, 0.0))
    r_i = lax.broadcasted_iota(I32, (tm, tm), 0)
    c_i = lax.broadcasted_iota(I32, (tm, tm), 1)
    strictly_lower = jnp.where(c_i < r_i, 1.0, 0.0).astype(BF16)
    before = _dot(strictly_lower, cnt.astype(BF16)) + carry_scr[...]
    rank1 = jnp.sum(jnp.where(hit1, before, 0.0), axis=-1, keepdims=True).astype(I32)
    rank2 = jnp.sum(jnp.where(hit2, before, 0.0), axis=-1, keepdims=True).astype(I32)
    carry_scr[...] += jnp.sum(cnt, axis=0, keepdims=True)

    idx_ref[...] = jnp.where(lane == 0, i1, jnp.where(lane == 1, i2,
                             jnp.where(lane == 2, rank1, jnp.where(lane == 3, rank2, 0))))
    wgt_ref[...] = jnp.where(lane == 0, w1, jnp.where(lane == 1, w2, 0.0))
    cnt_ref[...] = carry_scr[...].astype(I32)


def _router(h, g_all, wr_all, layer, j):
    t = h.shape[0]
    tm = ROW_TILE
    return pl.pallas_call(
        _router_body,
        grid=(t // tm,),
        in_specs=[
            pl.BlockSpec((tm, D_MODEL), lambda i: (i, 0)),
            pl.BlockSpec((None, 1, D_MODEL), lambda i: (layer, 0, 0)),
            pl.BlockSpec((None, D_MODEL, LANES), lambda i: (j, 0, 0)),
        ],
        out_specs=[pl.BlockSpec((tm, LANES), lambda i: (i, 0)),
                   pl.BlockSpec((tm, LANES), lambda i: (i, 0)),
                   pl.BlockSpec((1, LANES), lambda i: (0, 0))],
        out_shape=[jax.ShapeDtypeStruct((t, LANES), I32),
                   jax.ShapeDtypeStruct((t, LANES), F32),
                   jax.ShapeDtypeStruct((1, LANES), I32)],
        scratch_shapes=[pltpu.VMEM((1, LANES), F32)],
        compiler_params=_params(("arbitrary",), 32),
        name="router",
    )(h, g_all, wr_all)


def _routing_plan(idx, cnt, n_tiles):
    e1, e2, r1, r2 = idx[:, 0], idx[:, 1], idx[:, 2], idx[:, 3]
    counts = cnt[0, :N_EXPERTS]
    gsz = ((counts + MOE_TILE - 1) // MOE_TILE) * MOE_TILE
    gend = jnp.cumsum(gsz)
    gstart = gend - gsz
    pos1 = gstart[e1] + r1
    pos2 = gstart[e2] + r2
    n_used = gend[-1] // MOE_TILE
    tile_ids = jnp.arange(n_tiles, dtype=I32)
    texp = jnp.sum((tile_ids * MOE_TILE)[:, None] >= gend[None, :], axis=1).astype(I32)
    texp = jnp.minimum(texp, N_EXPERTS - 1)
    texp = jnp.where(tile_ids < n_used, texp, texp[jnp.maximum(n_used - 1, 0)])
    zrow = jnp.where(counts > 0, gend - MOE_TILE, -1)
    return pos1.astype(I32), pos2.astype(I32), texp, n_used.astype(I32).reshape(1), zrow.astype(I32)


TOKEN_ROWS = D_MODEL // LANES
assert TOKEN_ROWS == SUBLANES


def _token_tile(ref, token):
    return ref.at[pl.ds(pl.multiple_of(token * TOKEN_ROWS, TOKEN_ROWS), TOKEN_ROWS)]


def _store_token_major(dst_ref, x):
    n = x.shape[0]
    for c in range(TOKEN_ROWS):
        dst_ref[pl.ds(c, n, stride=TOKEN_ROWS), :] = x[:, c * LANES:(c + 1) * LANES]


def _load_token_major(src_ref, n):
    return jnp.concatenate([src_ref[pl.ds(c, n, stride=TOKEN_ROWS), :] for c in range(TOKEN_ROWS)],
                           axis=1)


def _dispatch_body(pos1_ref, pos2_ref, zrow_ref, h_ref, g_ref, xs_ref, hn_scr, sems):
    tm = h_ref.shape[0]
    i = pl.program_id(0)
    base = i * tm
    step_rows = tm * TOKEN_ROWS
    slot = i % 2
    buf = hn_scr.at[slot]

    @pl.when(i == 0)
    def _():
        zeros = hn_scr.at[1]
        zeros[...] = jnp.zeros(zeros.shape, F32)
        for e in range(N_EXPERTS):
            zrow = zrow_ref[e]

            @pl.when(zrow >= 0)
            def _():
                for part in range(MOE_TILE // tm):
                    row0 = pl.multiple_of((zrow + part * tm) * TOKEN_ROWS, TOKEN_ROWS)
                    cp = pltpu.make_async_copy(zeros, xs_ref.at[pl.ds(row0, step_rows)], sems.at[1, 0])
                    cp.start()
                    cp.wait()

    _store_token_major(buf, _rms(h_ref[...], g_ref[...]))

    def start(t, carry):
        src = _token_tile(buf, t)
        for k, pos_ref in enumerate((pos1_ref, pos2_ref)):
            pltpu.make_async_copy(src, _token_tile(xs_ref, pos_ref[base + t]),
                                  sems.at[slot, k]).start(priority=k)
        return carry

    lax.fori_loop(0, tm, start, 0, unroll=8)

    def drain(s):
        for k in range(TOP_K):
            pltpu.make_async_copy(hn_scr.at[s], xs_ref.at[pl.ds(0, step_rows)], sems.at[s, k]).wait()

    @pl.when(i > 0)
    def _():
        drain(1 - slot)

    @pl.when(i == pl.num_programs(0) - 1)
    def _():
        drain(slot)


def _dispatch(h, g_all, pos1, pos2, zrow, n_rows, layer):
    t = h.shape[0]
    tm = ROW_TILE
    return pl.pallas_call(
        _dispatch_body,
        grid_spec=pltpu.PrefetchScalarGridSpec(
            num_scalar_prefetch=3,
            grid=(t // tm,),
            in_specs=[pl.BlockSpec((tm, D_MODEL), lambda i, *_: (i, 0)),
                      pl.BlockSpec((None, 1, D_MODEL), lambda i, *_: (layer, 0, 0))],
            out_specs=_any_spec(),
            scratch_shapes=[pltpu.VMEM((2, tm * TOKEN_ROWS, LANES), F32),
                            pltpu.SemaphoreType.DMA((2, TOP_K))],
        ),
        out_shape=jax.ShapeDtypeStruct((n_rows * TOKEN_ROWS, LANES), F32),
        compiler_params=_params(("arbitrary",), 32),
        name="moe_dispatch",
    )(pos1, pos2, zrow, h, g_all)


def _moe_group_body(texp_ref, nused_ref, x_ref, wg_ref, wu_ref, wd_ref, y_ref, xb_ref, acc_ref):
    i = pl.program_id(0)
    f = pl.program_id(1)
    tm = xb_ref.shape[0]

    @pl.when(i < nused_ref[0])
    def _():
        @pl.when(f == 0)
        def _():
            xb_ref[...] = _load_token_major(x_ref, tm).astype(BF16)
            acc_ref[...] = jnp.zeros_like(acc_ref)

        xb = xb_ref[...]
        hid = jax.nn.silu(_dot(xb, wg_ref[...].astype(BF16))) * _dot(xb, wu_ref[...].astype(BF16))
        acc_ref[...] += _dot(hid.astype(BF16), wd_ref[...].astype(BF16))

        @pl.when(f == pl.num_programs(1) - 1)
        def _():
            _store_token_major(y_ref, acc_ref[...])


def _moe_group(xs, texp, n_used, wg, wu, wd, j):
    n_rows = xs.shape[0] // TOKEN_ROWS
    tm, tf = MOE_TILE, 512
    nf = D_FF_EXP // tf

    def row_blk(i, nu):
        return jnp.maximum(jnp.minimum(i, nu[0] - 1), 0)

    def f_blk(i, f, nu):
        return jnp.where(i < nu[0], f, nf - 1)

    return pl.pallas_call(
        _moe_group_body,
        grid_spec=pltpu.PrefetchScalarGridSpec(
            num_scalar_prefetch=2,
            grid=(n_rows // tm, nf),
            in_specs=[
                pl.BlockSpec((tm * TOKEN_ROWS, LANES), lambda i, f, te, nu: (row_blk(i, nu), 0)),
                pl.BlockSpec((None, None, D_MODEL, tf), lambda i, f, te, nu: (j, te[i], 0, f_blk(i, f, nu))),
                pl.BlockSpec((None, None, D_MODEL, tf), lambda i, f, te, nu: (j, te[i], 0, f_blk(i, f, nu))),
                pl.BlockSpec((None, None, tf, D_MODEL), lambda i, f, te, nu: (j, te[i], f_blk(i, f, nu), 0)),
            ],
            out_specs=pl.BlockSpec((tm * TOKEN_ROWS, LANES), lambda i, f, te, nu: (row_blk(i, nu), 0)),
            scratch_shapes=[pltpu.VMEM((tm, D_MODEL), BF16), pltpu.VMEM((tm, D_MODEL), F32)],
        ),
        out_shape=jax.ShapeDtypeStruct((n_rows * TOKEN_ROWS, LANES), F32),
        compiler_params=_params(("arbitrary", "arbitrary"), 52),
        name="moe_group",
    )(texp, n_used, xs, wg, wu, wd)


def _combine_body(pos1_ref, pos2_ref, h_ref, wgt_ref, gn_ref, ys_ref, *rest, last):
    *out_refs, y_scr, sems = rest
    tm = h_ref.shape[0]
    i = pl.program_id(0)
    slot = i % 2

    def fetch(step, s):
        base = step * tm

        def start(t, carry):
            for k, pos_ref in enumerate((pos1_ref, pos2_ref)):
                pltpu.make_async_copy(_token_tile(ys_ref, pos_ref[base + t]),
                                      _token_tile(y_scr.at[s, k], t), sems.at[s, k]).start(priority=k)
            return carry

        lax.fori_loop(0, tm, start, 0, unroll=8)

    @pl.when(i == 0)
    def _():
        fetch(0, 0)

    @pl.when(i + 1 < pl.num_programs(0))
    def _():
        fetch(i + 1, 1 - slot)

    for k in range(TOP_K):
        pltpu.make_async_copy(ys_ref.at[pl.ds(0, tm * TOKEN_ROWS)], y_scr.at[slot, k],
                              sems.at[slot, k]).wait()
    w = wgt_ref[...]
    out = h_ref[...] + (w[:, 0:1] * _load_token_major(y_scr.at[slot, 0], tm)
                        + w[:, 1:2] * _load_token_major(y_scr.at[slot, 1], tm))
    if last:
        out_refs[0][...] = _rms(out, gn_ref[...])
    else:
        out_refs[0][...] = out
        out_refs[1][...] = _rms(out, gn_ref[...]).astype(BF16)


def _combine(h, wgt, g_next, ys, pos1, pos2, last):
    t = h.shape[0]
    tm = ROW_TILE
    rows = pl.BlockSpec((tm, D_MODEL), lambda i, *_: (i, 0))
    full = jax.ShapeDtypeStruct((t, D_MODEL), F32)
    return pl.pallas_call(
        functools.partial(_combine_body, last=last),
        grid_spec=pltpu.PrefetchScalarGridSpec(
            num_scalar_prefetch=2,
            grid=(t // tm,),
            in_specs=[rows,
                      pl.BlockSpec((tm, LANES), lambda i, *_: (i, 0)),
                      pl.BlockSpec((1, D_MODEL), lambda i, *_: (0, 0)),
                      _any_spec()],
            out_specs=[rows] if last else [rows, rows],
            scratch_shapes=[pltpu.VMEM((2, TOP_K, tm * TOKEN_ROWS, LANES), F32),
                            pltpu.SemaphoreType.DMA((2, TOP_K))],
        ),
        out_shape=[full] if last else [full, jax.ShapeDtypeStruct((t, D_MODEL), BF16)],
        compiler_params=_params(("arbitrary",), 48),
        name="moe_combine",
    )(pos1, pos2, h, wgt, g_next, ys)


def _ffn_moe(h, g_all, g_next, wr_all, wg, wu, wd, layer, j, last):
    t = h.shape[0]
    n_rows = TOP_K * t + N_EXPERTS * MOE_TILE
    idx, wgt, cnt = _router(h, g_all, wr_all, layer, j)
    pos1, pos2, texp, n_used, zrow = _routing_plan(idx, cnt, n_rows // MOE_TILE)
    xs = _dispatch(h, g_all, pos1, pos2, zrow, n_rows, layer)
    ys = _moe_group(xs, texp, n_used, wg, wu, wd, j)
    return _combine(h, wgt, g_next, ys, pos1, pos2, last)


def _rope_tables(pos):
    half = DK_RET // 2
    inv = ROPE_BASE ** (-jnp.arange(half, dtype=F32) / half)
    ang = pos.astype(F32)[:, None] * inv[None, :]
    cos, sin = jnp.cos(ang), jnp.sin(ang)
    return jnp.concatenate([cos, cos], axis=-1), jnp.concatenate([-sin, sin], axis=-1)


def _decay_tables(c):
    log_g = jnp.log1p(-jnp.exp2(-5.0 - jnp.arange(H_RET, dtype=F32)))
    idx = jnp.arange(c, dtype=F32)
    diff = idx[:, None] - idx[None, :]
    causal = diff >= 0
    d_in = jnp.where(causal[None], jnp.exp(log_g[:, None, None] * jnp.where(causal, diff, 0.0)[None]), 0.0)
    d_q = jnp.exp(log_g[:, None] * (idx[None, :] + 1.0))
    d_k = jnp.exp(log_g[:, None] * (c - 1.0 - idx[None, :]))
    d_c = jnp.exp(log_g * c)
    return d_in, d_q, d_k, d_c


def _prompt_tables(seq):
    c = min(CHUNK, seq)
    cos2, sin2 = _rope_tables(jnp.arange(seq, dtype=I32))
    d_in, d_q, d_k, d_c = _decay_tables(c)
    return dict(cos=cos2, sin=sin2, din=d_in, dq=d_q[:, :, None], dk=d_k[:, :, None],
                dc=d_c[:, None, None])


def _sample_tables(seq, pos0):
    nb = SAMPLE_NB
    cos2, sin2 = _rope_tables(pos0 + jnp.arange(seq, dtype=I32))
    d_in, d_q, d_k, d_c = _decay_tables(seq)
    eye = jnp.eye(nb, dtype=F32)
    d_blk = jnp.einsum("ab,hij->haibj", eye, d_in).reshape(H_RET, nb * seq, nb * seq)
    return dict(cos=jnp.tile(cos2, (nb, 1)), sin=jnp.tile(sin2, (nb, 1)), din=d_blk,
                dq=jnp.tile(d_q, (1, nb))[:, :, None], dk=jnp.tile(d_k, (1, nb))[:, :, None],
                dc=d_c[:, None, None])


def kernel(x_prompt, x_sample, state_ret, state_pool, norm1_g, w_in, ln_v_g, ln_v_b, w_spatial, b_spatial, w_pool, pool_scale, w_br_a, w_br_b, w_br_c, w_out, norm2_g, w_ffn_gate, w_ffn_up, w_ffn_down, w_router, w_exp_gate, w_exp_up, w_exp_down, final_norm_g):
    batch, seq, _ = x_prompt.shape
    dec_batch, dec_seq, _ = x_sample.shape
    n_prompt = batch * seq
    n_sample = dec_batch * dec_seq
    assert seq % (PROMPT_CHUNKS_PER_STEP * CHUNK) == 0 and dec_seq <= CHUNK and PAST_LEN % CHUNK == 0
    assert n_prompt % (SAMPLE_NB * dec_seq) == 0 and dec_batch % SAMPLE_NB == 0

    lw = dict(
        ln_g=ln_v_g[:, None, :], ln_b=ln_v_b[:, None, :],
        w_spatial=w_spatial,
        bmap_p=jnp.repeat(jnp.swapaxes(b_spatial, 1, 2), DG_A, axis=2),
        wmix_s=jnp.repeat(jnp.transpose(w_spatial[:, :, :dec_seq, :dec_seq], (0, 3, 2, 1)), DG_A, axis=3),
        w_pool=w_pool.astype(BF16), pool_scale=pool_scale[:, None, :],
        w_br_a=w_br_a.astype(BF16), w_br_b=w_br_b.astype(BF16), w_br_c=w_br_c.astype(BF16),
        w_out=w_out.astype(BF16),
    )
    lw["bmap_s"] = lw["bmap_p"][:, :dec_seq, :]
    g1 = norm1_g[:, None, :]
    g2 = norm2_g[:, None, :]
    wfg, wfu, wfd = w_ffn_gate.astype(BF16), w_ffn_up.astype(BF16), w_ffn_down.astype(BF16)
    wr = jnp.pad(w_router, ((0, 0), (0, 0), (0, LANES - N_EXPERTS)))
    tabs_p = _prompt_tables(seq)
    tabs_s = _sample_tables(dec_seq, PAST_LEN)

    h = jnp.concatenate([x_prompt.reshape(n_prompt, D_MODEL),
                         x_sample.reshape(n_sample, D_MODEL)], axis=0)
    ret_p, pool_p, pool_s, v_s = [], [], [], []
    ret_s = None
    assert DEPTH % 2 == 0
    xn = _norm_cast(h, g1, 0)
    for layer in range(DEPTH):
        j = layer // 2
        last = layer == DEPTH - 1
        z = _inproj(xn, w_in, layer, gates=False)
        sg = _inproj(xn, w_in, layer, gates=True)
        a, o, c, s_p, b_p = _mix_prompt(z, batch, seq, tabs_p, lw, layer)
        a, o, c, vn, ret_s, b_s = _mix_sample(z, a, o, c, ret_s, n_prompt, dec_batch, dec_seq, PAST_LEN,
                                              tabs_s, lw, state_ret, state_pool, layer)
        h = _merge(a, o, c, sg, h, lw, layer)
        if layer % 2 == 0:
            h, xn = _ffn_dense(h, g2, g1, wfg, wfu, wfd, layer, j)
        else:
            g_next = final_norm_g[None, :] if last else g1[layer + 1]
            res = _ffn_moe(h, g2, g_next, wr, w_exp_gate, w_exp_up, w_exp_down, layer, j, last)
            if last:
                y, = res
            else:
                h, xn = res
        ret_p.append(s_p)
        pool_p.append(b_p)
        pool_s.append(b_s)
        v_s.append(vn.reshape(dec_batch, dec_seq, D_A))

    y_prompt = y[:n_prompt].reshape(batch, seq, D_MODEL)
    y_sample = y[n_prompt:].reshape(dec_batch, dec_seq, D_MODEL)
    return (y_prompt, y_sample, jnp.stack(ret_p), ret_s, jnp.stack(pool_p),
            jnp.stack(pool_s), jnp.stack(v_s))
```

```python
import functools

import jax
import jax.numpy as jnp
from jax import lax
from jax.experimental import pallas as pl
from jax.experimental.pallas import tpu as pltpu

F32 = jnp.float32
BF16 = jnp.bfloat16
I32 = jnp.int32

D_MODEL = 1024
DEPTH = 4
PAST_LEN = 16384
D_A = D_MODEL // 2
CHUNK = 128
G_A = 4
DG_A = D_A // G_A
H_RET = 4
DK_RET = D_MODEL // 2 // H_RET
DV_RET = 2 * DK_RET
ROPE_BASE = 10000.0
D_C = D_MODEL // 2
POOL_WINDOWS = (2, 4, 8, 16)
DG_C = D_C // len(POOL_WINDOWS)
POOL_BUF = max(POOL_WINDOWS) - 1
D_FF = 2816
N_EXPERTS = 8
TOP_K = 2
D_FF_EXP = 3584
EPS = 1e-6
QK_W = H_RET * DK_RET
V_W = H_RET * DV_RET
IN_COLS = 2 * D_A + 2 * QK_W + 2 * V_W + D_C + 3 * D_MODEL

LANES = 128
SUBLANES = 8
MIB = 1024 * 1024

BLK_U, BLK_V, BLK_Q, BLK_K, BLK_XC = 0, 1, 2, 3, 8
BLK_VR, BLK_GR = 2, 3
MIX_COLS = 2 * D_A + 2 * QK_W + 2 * V_W + D_C

PROMPT_CHUNKS_PER_STEP = 4
MOE_TILE = 1024
COMBINE_TILE = 512
ROW_TILE = 1024


def _params(sem, vmem_mib):
    return pltpu.CompilerParams(dimension_semantics=sem, vmem_limit_bytes=vmem_mib * MIB)


def _rms(x, g):
    return x * lax.rsqrt(jnp.mean(x * x, axis=-1, keepdims=True) + EPS) * g


def _layer_norm(x, g, b):
    mu = jnp.mean(x, axis=-1, keepdims=True)
    xc = x - mu
    var = jnp.mean(xc * xc, axis=-1, keepdims=True)
    return xc * lax.rsqrt(var + EPS) * g + b


def _dot(a, b):
    return jnp.dot(a, b, preferred_element_type=F32)


def _dot_nt(a, b):
    return lax.dot_general(a, b, (((1,), (1,)), ((), ())), preferred_element_type=F32)


def _dot_tn(a, b):
    return lax.dot_general(a, b, (((0,), (0,)), ((), ())), preferred_element_type=F32)


def _rotary(x, cos2, sin2):
    return x * cos2 + pltpu.roll(x, DK_RET // 2, 1) * sin2


def _any_spec():
    return pl.BlockSpec(memory_space=pl.ANY)


def _norm_cast_body(xp_ref, xs_ref, g_ref, h_ref, o_ref, *, prompt_blocks):
    x = jnp.where(pl.program_id(0) < prompt_blocks, xp_ref[...], xs_ref[...])
    h_ref[...] = x
    o_ref[...] = _rms(x, g_ref[...]).astype(BF16)


def _norm_cast(xp, xs, g_all, layer):
    tm = 1024
    pb, sb = xp.shape[0] // tm, xs.shape[0] // tm
    t = xp.shape[0] + xs.shape[0]
    rows = pl.BlockSpec((tm, D_MODEL), lambda i: (i, 0))
    return pl.pallas_call(
        functools.partial(_norm_cast_body, prompt_blocks=pb),
        grid=(pb + sb,),
        in_specs=[pl.BlockSpec((tm, D_MODEL), lambda i: (jnp.minimum(i, pb - 1), 0)),
                  pl.BlockSpec((tm, D_MODEL), lambda i: (jnp.maximum(i - pb, 0), 0)),
                  pl.BlockSpec((None, 1, D_MODEL), lambda i: (layer, 0, 0))],
        out_specs=[rows, rows],
        out_shape=[jax.ShapeDtypeStruct((t, D_MODEL), F32), jax.ShapeDtypeStruct((t, D_MODEL), BF16)],
        compiler_params=_params(("arbitrary",), 40),
        name="norm_cast",
    )(xp, xs, g_all)


def _inproj_body(x_ref, w_ref, o_ref, wb_ref):
    @pl.when(pl.program_id(1) == 0)
    def _():
        wb_ref[...] = w_ref[...].astype(BF16)

    o_ref[...] = _dot(x_ref[...], wb_ref[...]).astype(o_ref.dtype)


def _inproj(xn, w_all, layer, gates):
    t = xn.shape[0]
    tm, tn = 1024, 1536
    col0, cols, dtype = (MIX_COLS // tn, IN_COLS - MIX_COLS, BF16) if gates else (0, MIX_COLS, F32)
    return pl.pallas_call(
        _inproj_body,
        grid=(cols // tn, t // tm),
        in_specs=[
            pl.BlockSpec((tm, D_MODEL), lambda j, i: (i, 0)),
            pl.BlockSpec((None, D_MODEL, tn), lambda j, i: (layer, 0, col0 + j)),
        ],
        out_specs=pl.BlockSpec((tm, tn), lambda j, i: (i, j)),
        out_shape=jax.ShapeDtypeStruct((t, cols), dtype),
        scratch_shapes=[pltpu.VMEM((D_MODEL, tn), BF16)],
        compiler_params=_params(("parallel", "arbitrary"), 48),
        name="inproj_gates" if gates else "inproj",
    )(xn, w_all)


def _mix_prompt_body(u_ref, v_ref, q_ref, k_ref, vr_ref, gr_ref, xc_ref, cos_ref, sin_ref,
                     lng_ref, lnb_ref, ws_ref, bmap_ref, din_ref, dq_ref, dk_ref, dc_ref,
                     wp_ref, ps_ref,
                     a_ref, o_ref, c_ref, s_out_ref, buf_out_ref,
                     s_scr, xx_scr):
    n = pl.program_id(1)
    c = CHUNK
    hist = 2 * SUBLANES

    @pl.when(n == 0)
    def _():
        s_scr[...] = jnp.zeros_like(s_scr)
        xx_scr[0:hist, :] = jnp.zeros((hist, D_C), F32)

    row = lax.broadcasted_iota(I32, (c, c), 0)
    col = lax.broadcasted_iota(I32, (c, c), 1)

    for sub in range(PROMPT_CHUNKS_PER_STEP):
        rs = slice(sub * c, (sub + 1) * c)

        u = jax.nn.gelu(u_ref[rs, :])
        v = _layer_norm(jax.nn.gelu(v_ref[rs, :]), lng_ref[...], lnb_ref[...])
        for g in range(G_A):
            sl = slice(g * DG_A, (g + 1) * DG_A)
            w = jnp.where(col <= row, ws_ref[g], 0.0).astype(BF16)
            mixed = _dot(w, v[:, sl].astype(BF16)) + bmap_ref[:, sl]
            a_ref[rs, sl] = (u[:, sl] * mixed).astype(BF16)

        cos2 = cos_ref[rs, :]
        sin2 = sin_ref[rs, :]
        for h in range(H_RET):
            qs = slice(h * DK_RET, (h + 1) * DK_RET)
            vs = slice(h * DV_RET, (h + 1) * DV_RET)
            qr = _rotary(q_ref[rs, qs], cos2, sin2)
            kr = _rotary(k_ref[rs, qs], cos2, sin2) * (DK_RET ** -0.5)
            qb = qr.astype(BF16)
            vb = vr_ref[rs, vs].astype(BF16)
            scores = _dot_nt(qb, kr.astype(BF16)) * din_ref[h]
            inner = _dot(scores.astype(BF16), vb)
            s_old = s_scr[h]
            cross = _dot(qb, s_old.astype(BF16)) * dq_ref[h]
            kd = (kr * dk_ref[h]).astype(BF16)
            s_scr[h] = dc_ref[h] * s_old + _dot_tn(kd, vb)
            oh = inner + cross
            oh = oh * lax.rsqrt(jnp.mean(oh * oh, axis=-1, keepdims=True) + EPS)
            o_ref[rs, vs] = (jax.nn.silu(gr_ref[rs, vs]) * oh).astype(BF16)

        xc = xc_ref[rs, :]
        xx_scr[hist:hist + c, :] = xc
        pos1 = (n * PROMPT_CHUNKS_PER_STEP + sub) * c + 1 + lax.broadcasted_iota(I32, (c, DG_C), 0)
        for gi, w in enumerate(POOL_WINDOWS):
            sl = slice(gi * DG_C, (gi + 1) * DG_C)
            acc = xx_scr[hist:hist + c, sl]
            for j in range(1, w):
                acc = acc + xx_scr[hist - j:hist - j + c, sl]
            cnt = jnp.minimum(w, pos1).astype(F32)
            pooled = acc / cnt - xc[:, sl]
            c_ref[rs, sl] = (_dot(pooled.astype(BF16), wp_ref[gi]) * ps_ref[:, sl]).astype(BF16)
        xx_scr[0:hist, :] = xx_scr[c:c + hist, :]

    @pl.when(n == pl.num_programs(1) - 1)
    def _():
        s_out_ref[...] = s_scr[...]
        buf_out_ref[...] = xx_scr[hist - POOL_BUF:hist, :]


def _mix_prompt(z, batch, seq, tabs, lw, layer):
    c = CHUNK
    rows = PROMPT_CHUNKS_PER_STEP * c
    nc = seq // rows
    t = z.shape[0]

    def zspec(width, blk):
        return pl.BlockSpec((rows, width), lambda b, n: (b * nc + n, blk))

    def whole(shape):
        nd = len(shape)
        return pl.BlockSpec((None,) + shape, lambda b, n: (layer,) + (0,) * nd)

    def const(shape):
        nd = len(shape)
        return pl.BlockSpec(shape, lambda b, n: (0,) * nd)

    in_specs = [
        zspec(D_A, BLK_U), zspec(D_A, BLK_V), zspec(QK_W, BLK_Q), zspec(QK_W, BLK_K),
        zspec(V_W, BLK_VR), zspec(V_W, BLK_GR), zspec(D_C, BLK_XC),
        pl.BlockSpec((rows, DK_RET), lambda b, n: (n, 0)),
        pl.BlockSpec((rows, DK_RET), lambda b, n: (n, 0)),
        whole((1, D_A)), whole((1, D_A)), whole((G_A, c, c)), whole((c, D_A)),
        const((H_RET, c, c)), const((H_RET, c, 1)), const((H_RET, c, 1)), const((H_RET, 1, 1)),
        whole((len(POOL_WINDOWS), DG_C, DG_C)), whole((1, D_C)),
    ]
    out_specs = [
        pl.BlockSpec((rows, D_A), lambda b, n: (b * nc + n, 0)),
        pl.BlockSpec((rows, V_W), lambda b, n: (b * nc + n, 0)),
        pl.BlockSpec((rows, D_C), lambda b, n: (b * nc + n, 0)),
        pl.BlockSpec((None, H_RET, DK_RET, DV_RET), lambda b, n: (b, 0, 0, 0)),
        pl.BlockSpec((None, POOL_BUF, D_C), lambda b, n: (b, 0, 0)),
    ]
    out_shape = [
        jax.ShapeDtypeStruct((t, D_A), BF16),
        jax.ShapeDtypeStruct((t, V_W), BF16),
        jax.ShapeDtypeStruct((t, D_C), BF16),
        jax.ShapeDtypeStruct((batch, H_RET, DK_RET, DV_RET), F32),
        jax.ShapeDtypeStruct((batch, POOL_BUF, D_C), F32),
    ]
    return pl.pallas_call(
        _mix_prompt_body,
        grid=(batch, nc),
        in_specs=in_specs,
        out_specs=out_specs,
        out_shape=out_shape,
        scratch_shapes=[pltpu.VMEM((H_RET, DK_RET, DV_RET), F32),
                        pltpu.VMEM((c + 2 * SUBLANES, D_C), F32)],
        compiler_params=_params(("arbitrary", "arbitrary"), 32),
        name="mix_prompt",
    )(z, z, z, z, z, z, z, tabs["cos"], tabs["sin"],
      lw["ln_g"], lw["ln_b"], lw["w_spatial"], lw["bmap_p"],
      tabs["din"], tabs["dq"], tabs["dk"], tabs["dc"],
      lw["w_pool"], lw["pool_scale"])


SAMPLE_NB = 8


def _mix_sample_body(u_ref, v_ref, q_ref, k_ref, vr_ref, gr_ref, xc_ref, cos_ref, sin_ref,
                     lng_ref, lnb_ref, wmix_ref, bmap_ref, din_ref, dq_ref, dk_ref, dc_ref,
                     wp_ref, ps_ref, s_in_ref, buf_in_ref, *rest, seq, pos0, n_alias):
    a_ref, o_ref, c_ref, vn_ref, s_out_ref, buf_out_ref, xx_scr = rest[n_alias:]
    nb = SAMPLE_NB
    r = nb * seq
    hist = 2 * SUBLANES

    u = jax.nn.gelu(u_ref[...])
    v = _layer_norm(jax.nn.gelu(v_ref[...]), lng_ref[...], lnb_ref[...])
    vn_ref[...] = v
    v3 = v.reshape(nb, seq, D_A)
    t_idx = lax.broadcasted_iota(I32, (seq, D_A), 0)
    mixed = jnp.broadcast_to(bmap_ref[...][None], (nb, seq, D_A))
    for s in range(seq):
        w_s = jnp.where(t_idx >= s, wmix_ref[s], 0.0)
        mixed = mixed + w_s[None] * v3[:, s:s + 1, :]
    a_ref[...] = (u * mixed.reshape(r, D_A)).astype(BF16)

    cos2 = cos_ref[...]
    sin2 = sin_ref[...]
    rowb = lax.broadcasted_iota(I32, (r, DV_RET), 0) // seq
    for h in range(H_RET):
        qs = slice(h * DK_RET, (h + 1) * DK_RET)
        vs = slice(h * DV_RET, (h + 1) * DV_RET)
        qr = _rotary(q_ref[:, qs], cos2, sin2)
        kr = _rotary(k_ref[:, qs], cos2, sin2) * (DK_RET ** -0.5)
        qb = qr.astype(BF16)
        vh = vr_ref[:, vs]
        scores = _dot_nt(qb, kr.astype(BF16)) * din_ref[h]
        inner = _dot(scores.astype(BF16), vh.astype(BF16))
        kd = (kr * dk_ref[h]).astype(BF16)
        cross = jnp.zeros((r, DV_RET), F32)
        for b in range(nb):
            s_old = s_in_ref[b, h]
            cross = jnp.where(rowb == b, _dot(qb, s_old.astype(BF16)), cross)
            vm = jnp.where(rowb == b, vh, 0.0).astype(BF16)
            s_out_ref[b, h] = dc_ref[h] * s_old + _dot_tn(kd, vm)
        oh = inner + cross * dq_ref[h]
        oh = oh * lax.rsqrt(jnp.mean(oh * oh, axis=-1, keepdims=True) + EPS)
        o_ref[:, vs] = (jax.nn.silu(gr_ref[:, vs]) * oh).astype(BF16)

    xc = xc_ref[...]
    xx_scr[:, hist - POOL_BUF:hist, :] = buf_in_ref[...]
    xx_scr[:, hist:hist + seq, :] = xc.reshape(nb, seq, D_C)
    pos1 = pos0 + 1 + lax.broadcasted_iota(I32, (seq, DG_C), 0)
    for gi, w in enumerate(POOL_WINDOWS):
        sl = slice(gi * DG_C, (gi + 1) * DG_C)
        acc = xx_scr[:, hist:hist + seq, sl]
        for j in range(1, w):
            acc = acc + xx_scr[:, hist - j:hist - j + seq, sl]
        cnt = jnp.minimum(w, pos1).astype(F32)
        pooled = (acc / cnt[None]).reshape(r, DG_C) - xc[:, sl]
        c_ref[:, sl] = (_dot(pooled.astype(BF16), wp_ref[gi]) * ps_ref[:, sl]).astype(BF16)
    buf_out_ref[...] = xx_scr[:, hist + seq - POOL_BUF:hist + seq, :]


def _mix_sample(z, a_buf, o_buf, c_buf, ret_buf, row0, batch, seq, pos0, tabs, lw,
                state_ret, state_pool, layer):
    nb = SAMPLE_NB
    r = nb * seq
    blk0 = row0 // r
    rows = batch * seq
    t = z.shape[0]

    def zspec(width, blk):
        return pl.BlockSpec((r, width), lambda i: (blk0 + i, blk))

    def whole(shape):
        nd = len(shape)
        return pl.BlockSpec((None,) + shape, lambda i: (layer,) + (0,) * nd)

    def const(shape):
        nd = len(shape)
        return pl.BlockSpec(shape, lambda i: (0,) * nd)

    in_specs = [
        zspec(D_A, BLK_U), zspec(D_A, BLK_V), zspec(QK_W, BLK_Q), zspec(QK_W, BLK_K),
        zspec(V_W, BLK_VR), zspec(V_W, BLK_GR), zspec(D_C, BLK_XC),
        const((r, DK_RET)), const((r, DK_RET)),
        whole((1, D_A)), whole((1, D_A)), whole((seq, seq, D_A)), whole((seq, D_A)),
        const((H_RET, r, r)), const((H_RET, r, 1)), const((H_RET, r, 1)), const((H_RET, 1, 1)),
        whole((len(POOL_WINDOWS), DG_C, DG_C)), whole((1, D_C)),
        pl.BlockSpec((None, nb, H_RET, DK_RET, DV_RET), lambda i: (layer, i, 0, 0, 0)),
        pl.BlockSpec((None, nb, POOL_BUF, D_C), lambda i: (layer, i, 0, 0)),
    ]
    args = [z, z, z, z, z, z, z, tabs["cos"], tabs["sin"],
            lw["ln_g"], lw["ln_b"], lw["wmix_s"], lw["bmap_s"],
            tabs["din"], tabs["dq"], tabs["dk"], tabs["dc"],
            lw["w_pool"], lw["pool_scale"], state_ret, state_pool]
    n_in = len(args)
    aliased = [a_buf, o_buf, c_buf] + ([ret_buf] if ret_buf is not None else [])
    aliases = {n_in + 0: 0, n_in + 1: 1, n_in + 2: 2}
    if ret_buf is not None:
        aliases[n_in + 3] = 4
    in_specs += [_any_spec()] * len(aliased)
    out_specs = [
        pl.BlockSpec((r, D_A), lambda i: (blk0 + i, 0)),
        pl.BlockSpec((r, V_W), lambda i: (blk0 + i, 0)),
        pl.BlockSpec((r, D_C), lambda i: (blk0 + i, 0)),
        pl.BlockSpec((r, D_A), lambda i: (i, 0)),
        pl.BlockSpec((None, nb, H_RET, DK_RET, DV_RET), lambda i: (layer, i, 0, 0, 0)),
        pl.BlockSpec((nb, POOL_BUF, D_C), lambda i: (i, 0, 0)),
    ]
    out_shape = [
        jax.ShapeDtypeStruct((t, D_A), BF16),
        jax.ShapeDtypeStruct((t, V_W), BF16),
        jax.ShapeDtypeStruct((t, D_C), BF16),
        jax.ShapeDtypeStruct((rows, D_A), F32),
        jax.ShapeDtypeStruct((DEPTH, batch, H_RET, DK_RET, DV_RET), F32),
        jax.ShapeDtypeStruct((batch, POOL_BUF, D_C), F32),
    ]
    return pl.pallas_call(
        functools.partial(_mix_sample_body, seq=seq, pos0=pos0, n_alias=len(aliased)),
        grid=(batch // nb,),
        in_specs=in_specs,
        out_specs=out_specs,
        out_shape=out_shape,
        scratch_shapes=[pltpu.VMEM((nb, 2 * SUBLANES + seq, D_C), F32)],
        input_output_aliases=aliases,
        compiler_params=_params(("parallel",), 40),
        name="mix_sample",
    )(*args, *aliased)


def _merge_body(a_ref, o_ref, c_ref, ga_ref, gb_ref, gc_ref, h_ref,
                wa_ref, wb_ref, wc_ref, wo_ref, out_ref):
    def gate(g_ref):
        return jax.nn.sigmoid(g_ref[...].astype(F32))

    m = gate(ga_ref) * _dot(a_ref[...], wa_ref[...])
    m = m + gate(gb_ref) * _dot(o_ref[...], wb_ref[...])
    m = m + gate(gc_ref) * _dot(c_ref[...], wc_ref[...])
    out_ref[...] = h_ref[...] + _dot(m.astype(BF16), wo_ref[...])


def _merge(a, o, c, sg, h, lw, layer):
    t = h.shape[0]
    tm = 512

    def rows(width, blk=0):
        return pl.BlockSpec((tm, width), lambda i: (i, blk))

    def whole(shape):
        nd = len(shape)
        return pl.BlockSpec((None,) + shape, lambda i: (layer,) + (0,) * nd)

    return pl.pallas_call(
        _merge_body,
        grid=(t // tm,),
        in_specs=[rows(D_A), rows(V_W), rows(D_C),
                  rows(D_MODEL, 0), rows(D_MODEL, 1), rows(D_MODEL, 2),
                  rows(D_MODEL),
                  whole((D_A, D_MODEL)), whole((V_W, D_MODEL)), whole((D_C, D_MODEL)),
                  whole((D_MODEL, D_MODEL))],
        out_specs=pl.BlockSpec((tm, D_MODEL), lambda i: (i, 0)),
        out_shape=jax.ShapeDtypeStruct((t, D_MODEL), F32),
        compiler_params=_params(("parallel",), 48),
        name="merge",
    )(a, o, c, sg, sg, sg, h, lw["w_br_a"], lw["w_br_b"], lw["w_br_c"], lw["w_out"])


def _ffn_body(x_ref, g_ref, gn_ref, wg_ref, wu_ref, wd_ref, out_ref, xnext_ref, xn_ref, acc_ref):
    j = pl.program_id(1)

    @pl.when(j == 0)
    def _():
        xn_ref[...] = _rms(x_ref[...], g_ref[...]).astype(BF16)
        acc_ref[...] = jnp.zeros_like(acc_ref)

    xn = xn_ref[...]
    hid = jax.nn.silu(_dot(xn, wg_ref[...])) * _dot(xn, wu_ref[...])
    acc_ref[...] += _dot(hid.astype(BF16), wd_ref[...])

    @pl.when(j == pl.num_programs(1) - 1)
    def _():
        out = x_ref[...] + acc_ref[...]
        out_ref[...] = out
        xnext_ref[...] = _rms(out, gn_ref[...]).astype(BF16)


def _ffn_dense(h, g_all, g_next_all, wg, wu, wd, layer, j):
    t = h.shape[0]
    tm, tf = 512, 1408
    return pl.pallas_call(
        _ffn_body,
        grid=(t // tm, D_FF // tf),
        in_specs=[
            pl.BlockSpec((tm, D_MODEL), lambda i, f: (i, 0)),
            pl.BlockSpec((None, 1, D_MODEL), lambda i, f: (layer, 0, 0)),
            pl.BlockSpec((None, 1, D_MODEL), lambda i, f: (layer + 1, 0, 0)),
            pl.BlockSpec((None, D_MODEL, tf), lambda i, f: (j, 0, f)),
            pl.BlockSpec((None, D_MODEL, tf), lambda i, f: (j, 0, f)),
            pl.BlockSpec((None, tf, D_MODEL), lambda i, f: (j, f, 0)),
        ],
        out_specs=[pl.BlockSpec((tm, D_MODEL), lambda i, f: (i, 0)),
                   pl.BlockSpec((tm, D_MODEL), lambda i, f: (i, 0))],
        out_shape=[jax.ShapeDtypeStruct((t, D_MODEL), F32),
                   jax.ShapeDtypeStruct((t, D_MODEL), BF16)],
        scratch_shapes=[pltpu.VMEM((tm, D_MODEL), BF16), pltpu.VMEM((tm, D_MODEL), F32)],
        compiler_params=_params(("parallel", "arbitrary"), 48),
        name="ffn_dense",
    )(h, g_all, g_next_all, wg, wu, wd)


def _router_body(x_ref, g_ref, wr_ref, idx_ref, wgt_ref, cnt_ref, carry_scr):
    @pl.when(pl.program_id(0) == 0)
    def _():
        carry_scr[...] = jnp.zeros_like(carry_scr)

    tm = x_ref.shape[0]
    xn = _rms(x_ref[...], g_ref[...])
    logits = jnp.dot(xn, wr_ref[...], preferred_element_type=F32, precision=lax.Precision.HIGHEST)
    lane = lax.broadcasted_iota(I32, logits.shape, 1)
    neg = jnp.float32(-jnp.inf)
    logits = jnp.where(lane < N_EXPERTS, logits, neg)
    v1 = jnp.max(logits, axis=-1, keepdims=True)
    i1 = jnp.min(jnp.where(logits == v1, lane, LANES), axis=-1, keepdims=True)
    rest = jnp.where(lane == i1, neg, logits)
    v2 = jnp.max(rest, axis=-1, keepdims=True)
    i2 = jnp.min(jnp.where(rest == v2, lane, LANES), axis=-1, keepdims=True)
    e2 = jnp.exp(v2 - v1)
    den = 1.0 + e2
    w1 = 1.0 / den
    w2 = e2 / den

    hit1 = lane == i1
    hit2 = lane == i2
    cnt = jnp.where(hit1, 1.0, jnp.where(hit2, 1.0, 0.0))
    r_i = lax.broadcasted_iota(I32, (tm, tm), 0)
    c_i = lax.broadcasted_iota(I32, (tm, tm), 1)
    strictly_lower = jnp.where(c_i < r_i, 1.0, 0.0).astype(BF16)
    before = _dot(strictly_lower, cnt.astype(BF16)) + carry_scr[...]
    rank1 = jnp.sum(jnp.where(hit1, before, 0.0), axis=-1, keepdims=True).astype(I32)
    rank2 = jnp.sum(jnp.where(hit2, before, 0.0), axis=-1, keepdims=True).astype(I32)
    carry_scr[...] += jnp.sum(cnt, axis=0, keepdims=True)

    idx_ref[...] = jnp.where(lane == 0, i1, jnp.where(lane == 1, i2,
                             jnp.where(lane == 2, rank1, jnp.where(lane == 3, rank2, 0))))
    wgt_ref[...] = jnp.where(lane == 0, w1, jnp.where(lane == 1, w2, 0.0))
    cnt_ref[...] = carry_scr[...].astype(I32)


def _router(h, g_all, wr_all, layer, j):
    t = h.shape[0]
    tm = ROW_TILE
    return pl.pallas_call(
        _router_body,
        grid=(t // tm,),
        in_specs=[
            pl.BlockSpec((tm, D_MODEL), lambda i: (i, 0)),
            pl.BlockSpec((None, 1, D_MODEL), lambda i: (layer, 0, 0)),
            pl.BlockSpec((None, D_MODEL, LANES), lambda i: (j, 0, 0)),
        ],
        out_specs=[pl.BlockSpec((tm, LANES), lambda i: (i, 0)),
                   pl.BlockSpec((tm, LANES), lambda i: (i, 0)),
                   pl.BlockSpec((1, LANES), lambda i: (0, 0))],
        out_shape=[jax.ShapeDtypeStruct((t, LANES), I32),
                   jax.ShapeDtypeStruct((t, LANES), F32),
                   jax.ShapeDtypeStruct((1, LANES), I32)],
        scratch_shapes=[pltpu.VMEM((1, LANES), F32)],
        compiler_params=_params(("arbitrary",), 32),
        name="router",
    )(h, g_all, wr_all)


def _routing_plan(idx, cnt, n_tiles):
    e1, e2, r1, r2 = idx[:, 0], idx[:, 1], idx[:, 2], idx[:, 3]
    counts = cnt[0, :N_EXPERTS]
    gsz = ((counts + MOE_TILE - 1) // MOE_TILE) * MOE_TILE
    gend = jnp.cumsum(gsz)
    gstart = gend - gsz
    pos1 = gstart[e1] + r1
    pos2 = gstart[e2] + r2
    n_used = gend[-1] // MOE_TILE
    tile_ids = jnp.arange(n_tiles, dtype=I32)
    texp = jnp.sum((tile_ids * MOE_TILE)[:, None] >= gend[None, :], axis=1).astype(I32)
    texp = jnp.minimum(texp, N_EXPERTS - 1)
    texp = jnp.where(tile_ids < n_used, texp, texp[jnp.maximum(n_used - 1, 0)])
    zrow = jnp.where(counts > 0, gend - MOE_TILE, -1)
    return pos1.astype(I32), pos2.astype(I32), texp, n_used.astype(I32).reshape(1), zrow.astype(I32)


TOKEN_ROWS = D_MODEL // LANES
assert TOKEN_ROWS == SUBLANES


def _token_tile(ref, token):
    return ref.at[pl.ds(pl.multiple_of(token * TOKEN_ROWS, TOKEN_ROWS), TOKEN_ROWS)]


def _store_token_major(dst_ref, x):
    n = x.shape[0]
    for c in range(TOKEN_ROWS):
        dst_ref[pl.ds(c, n, stride=TOKEN_ROWS), :] = x[:, c * LANES:(c + 1) * LANES]


def _load_token_major(src_ref, n):
    return jnp.concatenate([src_ref[pl.ds(c, n, stride=TOKEN_ROWS), :] for c in range(TOKEN_ROWS)],
                           axis=1)


def _dispatch_body(pos1_ref, pos2_ref, zrow_ref, h_ref, g_ref, xs_ref, hn_scr, sems):
    tm = h_ref.shape[0]
    i = pl.program_id(0)
    base = i * tm
    step_rows = tm * TOKEN_ROWS
    slot = i % 2
    buf = hn_scr.at[slot]

    @pl.when(i == 0)
    def _():
        zeros = hn_scr.at[1]
        zeros[...] = jnp.zeros(zeros.shape, F32)
        for e in range(N_EXPERTS):
            zrow = zrow_ref[e]

            @pl.when(zrow >= 0)
            def _():
                for part in range(MOE_TILE // tm):
                    row0 = pl.multiple_of((zrow + part * tm) * TOKEN_ROWS, TOKEN_ROWS)
                    cp = pltpu.make_async_copy(zeros, xs_ref.at[pl.ds(row0, step_rows)], sems.at[1, 0])
                    cp.start()
                    cp.wait()

    _store_token_major(buf, _rms(h_ref[...], g_ref[...]))

    def start(t, carry):
        src = _token_tile(buf, t)
        for k, pos_ref in enumerate((pos1_ref, pos2_ref)):
            pltpu.make_async_copy(src, _token_tile(xs_ref, pos_ref[base + t]),
                                  sems.at[slot, k]).start(priority=k)
        return carry

    lax.fori_loop(0, tm, start, 0, unroll=8)

    def drain(s):
        for k in range(TOP_K):
            pltpu.make_async_copy(hn_scr.at[s], xs_ref.at[pl.ds(0, step_rows)], sems.at[s, k]).wait()

    @pl.when(i > 0)
    def _():
        drain(1 - slot)

    @pl.when(i == pl.num_programs(0) - 1)
    def _():
        drain(slot)


def _dispatch(h, g_all, pos1, pos2, zrow, n_rows, layer):
    t = h.shape[0]
    tm = ROW_TILE
    return pl.pallas_call(
        _dispatch_body,
        grid_spec=pltpu.PrefetchScalarGridSpec(
            num_scalar_prefetch=3,
            grid=(t // tm,),
            in_specs=[pl.BlockSpec((tm, D_MODEL), lambda i, *_: (i, 0)),
                      pl.BlockSpec((None, 1, D_MODEL), lambda i, *_: (layer, 0, 0))],
            out_specs=_any_spec(),
            scratch_shapes=[pltpu.VMEM((2, tm * TOKEN_ROWS, LANES), F32),
                            pltpu.SemaphoreType.DMA((2, TOP_K))],
        ),
        out_shape=jax.ShapeDtypeStruct((n_rows * TOKEN_ROWS, LANES), F32),
        compiler_params=_params(("arbitrary",), 32),
        name="moe_dispatch",
    )(pos1, pos2, zrow, h, g_all)


def _moe_group_body(texp_ref, nused_ref, x_ref, wg_ref, wu_ref, wd_ref, y_ref, xb_ref, acc_ref):
    i = pl.program_id(0)
    f = pl.program_id(1)
    tm = xb_ref.shape[0]

    @pl.when(i < nused_ref[0])
    def _():
        @pl.when(f == 0)
        def _():
            xb_ref[...] = _load_token_major(x_ref, tm).astype(BF16)
            acc_ref[...] = jnp.zeros_like(acc_ref)

        xb = xb_ref[...]
        hid = jax.nn.silu(_dot(xb, wg_ref[...].astype(BF16))) * _dot(xb, wu_ref[...].astype(BF16))
        acc_ref[...] += _dot(hid.astype(BF16), wd_ref[...].astype(BF16))

        @pl.when(f == pl.num_programs(1) - 1)
        def _():
            _store_token_major(y_ref, acc_ref[...])


def _moe_group(xs, texp, n_used, wg, wu, wd, j):
    n_rows = xs.shape[0] // TOKEN_ROWS
    tm, tf = MOE_TILE, 512
    nf = D_FF_EXP // tf

    def row_blk(i, nu):
        return jnp.maximum(jnp.minimum(i, nu[0] - 1), 0)

    def f_blk(i, f, nu):
        return jnp.where(i < nu[0], f, nf - 1)

    return pl.pallas_call(
        _moe_group_body,
        grid_spec=pltpu.PrefetchScalarGridSpec(
            num_scalar_prefetch=2,
            grid=(n_rows // tm, nf),
            in_specs=[
                pl.BlockSpec((tm * TOKEN_ROWS, LANES), lambda i, f, te, nu: (row_blk(i, nu), 0)),
                pl.BlockSpec((None, None, D_MODEL, tf), lambda i, f, te, nu: (j, te[i], 0, f_blk(i, f, nu))),
                pl.BlockSpec((None, None, D_MODEL, tf), lambda i, f, te, nu: (j, te[i], 0, f_blk(i, f, nu))),
                pl.BlockSpec((None, None, tf, D_MODEL), lambda i, f, te, nu: (j, te[i], f_blk(i, f, nu), 0)),
            ],
            out_specs=pl.BlockSpec((tm * TOKEN_ROWS, LANES), lambda i, f, te, nu: (row_blk(i, nu), 0)),
            scratch_shapes=[pltpu.VMEM((tm, D_MODEL), BF16), pltpu.VMEM((tm, D_MODEL), F32)],
        ),
        out_shape=jax.ShapeDtypeStruct((n_rows * TOKEN_ROWS, LANES), F32),
        compiler_params=_params(("arbitrary", "arbitrary"), 52),
        name="moe_group",
    )(texp, n_used, xs, wg, wu, wd)


def _combine_body(pos1_ref, pos2_ref, h_ref, wgt_ref, gn_ref, ys_ref, *rest, final_prompt_blocks):
    *out_refs, y_scr, sems = rest
    tm = h_ref.shape[0]
    i = pl.program_id(0)
    slot = i % 2

    def fetch(step, s):
        base = step * tm

        def start(t, carry):
            for k, pos_ref in enumerate((pos1_ref, pos2_ref)):
                pltpu.make_async_copy(_token_tile(ys_ref, pos_ref[base + t]),
                                      _token_tile(y_scr.at[s, k], t), sems.at[s, k]).start(priority=k)
            return carry

        lax.fori_loop(0, tm, start, 0, unroll=8)

    @pl.when(i == 0)
    def _():
        fetch(0, 0)

    @pl.when(i + 1 < pl.num_programs(0))
    def _():
        fetch(i + 1, 1 - slot)

    for k in range(TOP_K):
        pltpu.make_async_copy(ys_ref.at[pl.ds(0, tm * TOKEN_ROWS)], y_scr.at[slot, k],
                              sems.at[slot, k]).wait()
    w = wgt_ref[...]
    out = h_ref[...] + (w[:, 0:1] * _load_token_major(y_scr.at[slot, 0], tm)
                        + w[:, 1:2] * _load_token_major(y_scr.at[slot, 1], tm))
    if final_prompt_blocks is None:
        out_refs[0][...] = out
        out_refs[1][...] = _rms(out, gn_ref[...]).astype(BF16)
    else:
        y = _rms(out, gn_ref[...])

        @pl.when(i < final_prompt_blocks)
        def _():
            out_refs[0][...] = y

        @pl.when(i >= final_prompt_blocks)
        def _():
            out_refs[1][...] = y


def _combine(h, wgt, g_next, ys, pos1, pos2, final_prompt_rows):
    t = h.shape[0]
    tm = COMBINE_TILE
    rows = pl.BlockSpec((tm, D_MODEL), lambda i, *_: (i, 0))
    if final_prompt_rows is None:
        pb = None
        out_specs = [rows, rows]
        out_shape = [jax.ShapeDtypeStruct((t, D_MODEL), F32), jax.ShapeDtypeStruct((t, D_MODEL), BF16)]
    else:
        pb = final_prompt_rows // tm
        out_specs = [pl.BlockSpec((tm, D_MODEL), lambda i, *_: (jnp.minimum(i, pb - 1), 0)),
                     pl.BlockSpec((tm, D_MODEL), lambda i, *_: (jnp.maximum(i - pb, 0), 0))]
        out_shape = [jax.ShapeDtypeStruct((final_prompt_rows, D_MODEL), F32),
                     jax.ShapeDtypeStruct((t - final_prompt_rows, D_MODEL), F32)]
    return pl.pallas_call(
        functools.partial(_combine_body, final_prompt_blocks=pb),
        grid_spec=pltpu.PrefetchScalarGridSpec(
            num_scalar_prefetch=2,
            grid=(t // tm,),
            in_specs=[rows,
                      pl.BlockSpec((tm, LANES), lambda i, *_: (i, 0)),
                      pl.BlockSpec((1, D_MODEL), lambda i, *_: (0, 0)),
                      _any_spec()],
            out_specs=out_specs,
            scratch_shapes=[pltpu.VMEM((2, TOP_K, tm * TOKEN_ROWS, LANES), F32),
                            pltpu.SemaphoreType.DMA((2, TOP_K))],
        ),
        out_shape=out_shape,
        compiler_params=_params(("arbitrary",), 32),
        name="moe_combine",
    )(pos1, pos2, h, wgt, g_next, ys)


def _ffn_moe(h, g_all, g_next, wr_all, wg, wu, wd, layer, j, final_prompt_rows):
    t = h.shape[0]
    n_rows = TOP_K * t + N_EXPERTS * MOE_TILE
    idx, wgt, cnt = _router(h, g_all, wr_all, layer, j)
    pos1, pos2, texp, n_used, zrow = _routing_plan(idx, cnt, n_rows // MOE_TILE)
    xs = _dispatch(h, g_all, pos1, pos2, zrow, n_rows, layer)
    ys = _moe_group(xs, texp, n_used, wg, wu, wd, j)
    return _combine(h, wgt, g_next, ys, pos1, pos2, final_prompt_rows)


def _rope_tables(pos):
    half = DK_RET // 2
    inv = ROPE_BASE ** (-jnp.arange(half, dtype=F32) / half)
    ang = pos.astype(F32)[:, None] * inv[None, :]
    cos, sin = jnp.cos(ang), jnp.sin(ang)
    return jnp.concatenate([cos, cos], axis=-1), jnp.concatenate([-sin, sin], axis=-1)


def _decay_tables(c):
    log_g = jnp.log1p(-jnp.exp2(-5.0 - jnp.arange(H_RET, dtype=F32)))
    idx = jnp.arange(c, dtype=F32)
    diff = idx[:, None] - idx[None, :]
    causal = diff >= 0
    d_in = jnp.where(causal[None], jnp.exp(log_g[:, None, None] * jnp.where(causal, diff, 0.0)[None]), 0.0)
    d_q = jnp.exp(log_g[:, None] * (idx[None, :] + 1.0))
    d_k = jnp.exp(log_g[:, None] * (c - 1.0 - idx[None, :]))
    d_c = jnp.exp(log_g * c)
    return d_in, d_q, d_k, d_c


def _prompt_tables(seq):
    c = min(CHUNK, seq)
    cos2, sin2 = _rope_tables(jnp.arange(seq, dtype=I32))
    d_in, d_q, d_k, d_c = _decay_tables(c)
    return dict(cos=cos2, sin=sin2, din=d_in, dq=d_q[:, :, None], dk=d_k[:, :, None],
                dc=d_c[:, None, None])


def _sample_tables(seq, pos0):
    nb = SAMPLE_NB
    cos2, sin2 = _rope_tables(pos0 + jnp.arange(seq, dtype=I32))
    d_in, d_q, d_k, d_c = _decay_tables(seq)
    eye = jnp.eye(nb, dtype=F32)
    d_blk = jnp.einsum("ab,hij->haibj", eye, d_in).reshape(H_RET, nb * seq, nb * seq)
    return dict(cos=jnp.tile(cos2, (nb, 1)), sin=jnp.tile(sin2, (nb, 1)), din=d_blk,
                dq=jnp.tile(d_q, (1, nb))[:, :, None], dk=jnp.tile(d_k, (1, nb))[:, :, None],
                dc=d_c[:, None, None])


def kernel(x_prompt, x_sample, state_ret, state_pool, norm1_g, w_in, ln_v_g, ln_v_b, w_spatial, b_spatial, w_pool, pool_scale, w_br_a, w_br_b, w_br_c, w_out, norm2_g, w_ffn_gate, w_ffn_up, w_ffn_down, w_router, w_exp_gate, w_exp_up, w_exp_down, final_norm_g):
    batch, seq, _ = x_prompt.shape
    dec_batch, dec_seq, _ = x_sample.shape
    n_prompt = batch * seq
    n_sample = dec_batch * dec_seq
    assert seq % (PROMPT_CHUNKS_PER_STEP * CHUNK) == 0 and dec_seq <= CHUNK and PAST_LEN % CHUNK == 0
    assert n_prompt % (SAMPLE_NB * dec_seq) == 0 and dec_batch % SAMPLE_NB == 0

    lw = dict(
        ln_g=ln_v_g[:, None, :], ln_b=ln_v_b[:, None, :],
        w_spatial=w_spatial,
        bmap_p=jnp.repeat(jnp.swapaxes(b_spatial, 1, 2), DG_A, axis=2),
        wmix_s=jnp.repeat(jnp.transpose(w_spatial[:, :, :dec_seq, :dec_seq], (0, 3, 2, 1)), DG_A, axis=3),
        w_pool=w_pool.astype(BF16), pool_scale=pool_scale[:, None, :],
        w_br_a=w_br_a.astype(BF16), w_br_b=w_br_b.astype(BF16), w_br_c=w_br_c.astype(BF16),
        w_out=w_out.astype(BF16),
    )
    lw["bmap_s"] = lw["bmap_p"][:, :dec_seq, :]
    g1 = norm1_g[:, None, :]
    g2 = norm2_g[:, None, :]
    wfg, wfu, wfd = w_ffn_gate.astype(BF16), w_ffn_up.astype(BF16), w_ffn_down.astype(BF16)
    wr = jnp.pad(w_router, ((0, 0), (0, 0), (0, LANES - N_EXPERTS)))
    tabs_p = _prompt_tables(seq)
    tabs_s = _sample_tables(dec_seq, PAST_LEN)

    ret_p, pool_p, pool_s, v_s = [], [], [], []
    ret_s = None
    assert DEPTH % 2 == 0
    h, xn = _norm_cast(x_prompt.reshape(n_prompt, D_MODEL), x_sample.reshape(n_sample, D_MODEL), g1, 0)
    for layer in range(DEPTH):
        j = layer // 2
        last = layer == DEPTH - 1
        z = _inproj(xn, w_in, layer, gates=False)
        sg = _inproj(xn, w_in, layer, gates=True)
        a, o, c, s_p, b_p = _mix_prompt(z, batch, seq, tabs_p, lw, layer)
        a, o, c, vn, ret_s, b_s = _mix_sample(z, a, o, c, ret_s, n_prompt, dec_batch, dec_seq, PAST_LEN,
                                              tabs_s, lw, state_ret, state_pool, layer)
        h = _merge(a, o, c, sg, h, lw, layer)
        if layer % 2 == 0:
            h, xn = _ffn_dense(h, g2, g1, wfg, wfu, wfd, layer, j)
        else:
            g_next = final_norm_g[None, :] if last else g1[layer + 1]
            res = _ffn_moe(h, g2, g_next, wr, w_exp_gate, w_exp_up, w_exp_down, layer, j,
                           n_prompt if last else None)
            if last:
                y_p, y_s = res
            else:
                h, xn = res
        ret_p.append(s_p)
        pool_p.append(b_p)
        pool_s.append(b_s)
        v_s.append(vn.reshape(dec_batch, dec_seq, D_A))

    y_prompt = y_p.reshape(batch, seq, D_MODEL)
    y_sample = y_s.reshape(dec_batch, dec_seq, D_MODEL)
    return (y_prompt, y_sample, jnp.stack(ret_p), ret_s, jnp.stack(pool_p),
            jnp.stack(pool_s), jnp.stack(v_s))
```

```python
import functools

import jax
import jax.numpy as jnp
from jax import lax
from jax.experimental import pallas as pl
from jax.experimental.pallas import tpu as pltpu

F32 = jnp.float32
BF16 = jnp.bfloat16
I32 = jnp.int32

D_MODEL = 1024
DEPTH = 4
PAST_LEN = 16384
D_A = D_MODEL // 2
CHUNK = 128
G_A = 4
DG_A = D_A // G_A
H_RET = 4
DK_RET = D_MODEL // 2 // H_RET
DV_RET = 2 * DK_RET
ROPE_BASE = 10000.0
D_C = D_MODEL // 2
POOL_WINDOWS = (2, 4, 8, 16)
DG_C = D_C // len(POOL_WINDOWS)
POOL_BUF = max(POOL_WINDOWS) - 1
D_FF = 2816
N_EXPERTS = 8
TOP_K = 2
D_FF_EXP = 3584
EPS = 1e-6
QK_W = H_RET * DK_RET
V_W = H_RET * DV_RET
IN_COLS = 2 * D_A + 2 * QK_W + 2 * V_W + D_C + 3 * D_MODEL

LANES = 128
SUBLANES = 8
MIB = 1024 * 1024

BLK_U, BLK_V, BLK_Q, BLK_K, BLK_XC = 0, 1, 2, 3, 8
BLK_VR, BLK_GR = 2, 3
MIX_COLS = 2 * D_A + 2 * QK_W + 2 * V_W + D_C

PROMPT_CHUNKS_PER_STEP = 4
MOE_TILE = 1024
COMBINE_TILE = 512
ROW_TILE = 1024


def _params(sem, vmem_mib):
    return pltpu.CompilerParams(dimension_semantics=sem, vmem_limit_bytes=vmem_mib * MIB)


def _rms(x, g):
    return x * lax.rsqrt(jnp.mean(x * x, axis=-1, keepdims=True) + EPS) * g


def _layer_norm(x, g, b):
    mu = jnp.mean(x, axis=-1, keepdims=True)
    xc = x - mu
    var = jnp.mean(xc * xc, axis=-1, keepdims=True)
    return xc * lax.rsqrt(var + EPS) * g + b


def _dot(a, b):
    return jnp.dot(a, b, preferred_element_type=F32)


def _dot_nt(a, b):
    return lax.dot_general(a, b, (((1,), (1,)), ((), ())), preferred_element_type=F32)


def _dot_tn(a, b):
    return lax.dot_general(a, b, (((0,), (0,)), ((), ())), preferred_element_type=F32)


def _rotary(x, cos2, sin2):
    return x * cos2 + pltpu.roll(x, DK_RET // 2, 1) * sin2


def _any_spec():
    return pl.BlockSpec(memory_space=pl.ANY)


def _norm_cast_body(xp_ref, xs_ref, g_ref, h_ref, o_ref, *, prompt_blocks):
    x = jnp.where(pl.program_id(0) < prompt_blocks, xp_ref[...], xs_ref[...])
    h_ref[...] = x
    o_ref[...] = _rms(x, g_ref[...]).astype(BF16)


def _norm_cast(xp, xs, g_all, layer):
    tm = 1024
    pb, sb = xp.shape[0] // tm, xs.shape[0] // tm
    t = xp.shape[0] + xs.shape[0]
    rows = pl.BlockSpec((tm, D_MODEL), lambda i: (i, 0))
    return pl.pallas_call(
        functools.partial(_norm_cast_body, prompt_blocks=pb),
        grid=(pb + sb,),
        in_specs=[pl.BlockSpec((tm, D_MODEL), lambda i: (jnp.minimum(i, pb - 1), 0)),
                  pl.BlockSpec((tm, D_MODEL), lambda i: (jnp.maximum(i - pb, 0), 0)),
                  pl.BlockSpec((None, 1, D_MODEL), lambda i: (layer, 0, 0))],
        out_specs=[rows, rows],
        out_shape=[jax.ShapeDtypeStruct((t, D_MODEL), F32), jax.ShapeDtypeStruct((t, D_MODEL), BF16)],
        compiler_params=_params(("arbitrary",), 40),
        name="norm_cast",
    )(xp, xs, g_all)


def _inproj_body(x_ref, w_ref, o_ref, wb_ref):
    @pl.when(pl.program_id(1) == 0)
    def _():
        wb_ref[...] = w_ref[...].astype(BF16)

    o_ref[...] = _dot(x_ref[...], wb_ref[...]).astype(o_ref.dtype)


def _inproj(xn, w_all, layer, gates):
    t = xn.shape[0]
    tm, tn = 1024, 1536
    col0, cols, dtype = (MIX_COLS // tn, IN_COLS - MIX_COLS, BF16) if gates else (0, MIX_COLS, F32)
    return pl.pallas_call(
        _inproj_body,
        grid=(cols // tn, t // tm),
        in_specs=[
            pl.BlockSpec((tm, D_MODEL), lambda j, i: (i, 0)),
            pl.BlockSpec((None, D_MODEL, tn), lambda j, i: (layer, 0, col0 + j)),
        ],
        out_specs=pl.BlockSpec((tm, tn), lambda j, i: (i, j)),
        out_shape=jax.ShapeDtypeStruct((t, cols), dtype),
        scratch_shapes=[pltpu.VMEM((D_MODEL, tn), BF16)],
        compiler_params=_params(("parallel", "arbitrary"), 48),
        name="inproj_gates" if gates else "inproj",
    )(xn, w_all)


def _mix_prompt_body(u_ref, v_ref, q_ref, k_ref, vr_ref, gr_ref, xc_ref, cos_ref, sin_ref,
                     lng_ref, lnb_ref, ws_ref, bmap_ref, din_ref, dq_ref, dk_ref, dc_ref,
                     wp_ref, ps_ref,
                     a_ref, o_ref, c_ref, s_out_ref, buf_out_ref,
                     s_scr, xx_scr):
    n = pl.program_id(1)
    c = CHUNK
    hist = 2 * SUBLANES

    @pl.when(n == 0)
    def _():
        s_scr[...] = jnp.zeros_like(s_scr)
        xx_scr[0:hist, :] = jnp.zeros((hist, D_C), F32)

    row = lax.broadcasted_iota(I32, (c, c), 0)
    col = lax.broadcasted_iota(I32, (c, c), 1)

    for sub in range(PROMPT_CHUNKS_PER_STEP):
        rs = slice(sub * c, (sub + 1) * c)

        u = jax.nn.gelu(u_ref[rs, :])
        v = _layer_norm(jax.nn.gelu(v_ref[rs, :]), lng_ref[...], lnb_ref[...])
        for g in range(G_A):
            sl = slice(g * DG_A, (g + 1) * DG_A)
            w = jnp.where(col <= row, ws_ref[g], 0.0).astype(BF16)
            mixed = _dot(w, v[:, sl].astype(BF16)) + bmap_ref[:, sl]
            a_ref[rs, sl] = (u[:, sl] * mixed).astype(BF16)

        cos2 = cos_ref[rs, :]
        sin2 = sin_ref[rs, :]
        for h in range(H_RET):
            qs = slice(h * DK_RET, (h + 1) * DK_RET)
            vs = slice(h * DV_RET, (h + 1) * DV_RET)
            qr = _rotary(q_ref[rs, qs], cos2, sin2)
            kr = _rotary(k_ref[rs, qs], cos2, sin2) * (DK_RET ** -0.5)
            qb = qr.astype(BF16)
            vb = vr_ref[rs, vs].astype(BF16)
            scores = _dot_nt(qb, kr.astype(BF16)) * din_ref[h]
            inner = _dot(scores.astype(BF16), vb)
            s_old = s_scr[h]
            cross = _dot(qb, s_old.astype(BF16)) * dq_ref[h]
            kd = (kr * dk_ref[h]).astype(BF16)
            s_scr[h] = dc_ref[h] * s_old + _dot_tn(kd, vb)
            oh = inner + cross
            oh = oh * lax.rsqrt(jnp.mean(oh * oh, axis=-1, keepdims=True) + EPS)
            o_ref[rs, vs] = (jax.nn.silu(gr_ref[rs, vs]) * oh).astype(BF16)

        xc = xc_ref[rs, :]
        xx_scr[hist:hist + c, :] = xc
        pos1 = (n * PROMPT_CHUNKS_PER_STEP + sub) * c + 1 + lax.broadcasted_iota(I32, (c, DG_C), 0)
        for gi, w in enumerate(POOL_WINDOWS):
            sl = slice(gi * DG_C, (gi + 1) * DG_C)
            acc = xx_scr[hist:hist + c, sl]
            for j in range(1, w):
                acc = acc + xx_scr[hist - j:hist - j + c, sl]
            cnt = jnp.minimum(w, pos1).astype(F32)
            pooled = acc / cnt - xc[:, sl]
            c_ref[rs, sl] = (_dot(pooled.astype(BF16), wp_ref[gi]) * ps_ref[:, sl]).astype(BF16)
        xx_scr[0:hist, :] = xx_scr[c:c + hist, :]

    @pl.when(n == pl.num_programs(1) - 1)
    def _():
        s_out_ref[...] = s_scr[...]
        buf_out_ref[...] = xx_scr[hist - POOL_BUF:hist, :]


def _mix_prompt(z, batch, seq, tabs, lw, layer):
    c = CHUNK
    rows = PROMPT_CHUNKS_PER_STEP * c
    nc = seq // rows
    t = z.shape[0]

    def zspec(width, blk):
        return pl.BlockSpec((rows, width), lambda b, n: (b * nc + n, blk))

    def whole(shape):
        nd = len(shape)
        return pl.BlockSpec((None,) + shape, lambda b, n: (layer,) + (0,) * nd)

    def const(shape):
        nd = len(shape)
        return pl.BlockSpec(shape, lambda b, n: (0,) * nd)

    in_specs = [
        zspec(D_A, BLK_U), zspec(D_A, BLK_V), zspec(QK_W, BLK_Q), zspec(QK_W, BLK_K),
        zspec(V_W, BLK_VR), zspec(V_W, BLK_GR), zspec(D_C, BLK_XC),
        pl.BlockSpec((rows, DK_RET), lambda b, n: (n, 0)),
        pl.BlockSpec((rows, DK_RET), lambda b, n: (n, 0)),
        whole((1, D_A)), whole((1, D_A)), whole((G_A, c, c)), whole((c, D_A)),
        const((H_RET, c, c)), const((H_RET, c, 1)), const((H_RET, c, 1)), const((H_RET, 1, 1)),
        whole((len(POOL_WINDOWS), DG_C, DG_C)), whole((1, D_C)),
    ]
    out_specs = [
        pl.BlockSpec((rows, D_A), lambda b, n: (b * nc + n, 0)),
        pl.BlockSpec((rows, V_W), lambda b, n: (b * nc + n, 0)),
        pl.BlockSpec((rows, D_C), lambda b, n: (b * nc + n, 0)),
        pl.BlockSpec((None, H_RET, DK_RET, DV_RET), lambda b, n: (b, 0, 0, 0)),
        pl.BlockSpec((None, POOL_BUF, D_C), lambda b, n: (b, 0, 0)),
    ]
    out_shape = [
        jax.ShapeDtypeStruct((t, D_A), BF16),
        jax.ShapeDtypeStruct((t, V_W), BF16),
        jax.ShapeDtypeStruct((t, D_C), BF16),
        jax.ShapeDtypeStruct((batch, H_RET, DK_RET, DV_RET), F32),
        jax.ShapeDtypeStruct((batch, POOL_BUF, D_C), F32),
    ]
    return pl.pallas_call(
        _mix_prompt_body,
        grid=(batch, nc),
        in_specs=in_specs,
        out_specs=out_specs,
        out_shape=out_shape,
        scratch_shapes=[pltpu.VMEM((H_RET, DK_RET, DV_RET), F32),
                        pltpu.VMEM((c + 2 * SUBLANES, D_C), F32)],
        compiler_params=_params(("arbitrary", "arbitrary"), 32),
        name="mix_prompt",
    )(z, z, z, z, z, z, z, tabs["cos"], tabs["sin"],
      lw["ln_g"], lw["ln_b"], lw["w_spatial"], lw["bmap_p"],
      tabs["din"], tabs["dq"], tabs["dk"], tabs["dc"],
      lw["w_pool"], lw["pool_scale"])


SAMPLE_NB = 8


def _mix_sample_body(u_ref, v_ref, q_ref, k_ref, vr_ref, gr_ref, xc_ref, cos_ref, sin_ref,
                     lng_ref, lnb_ref, wmix_ref, bmap_ref, din_ref, dq_ref, dk_ref, dc_ref,
                     wp_ref, ps_ref, s_in_ref, buf_in_ref, *rest, seq, pos0, n_alias):
    a_ref, o_ref, c_ref, vn_ref, s_out_ref, buf_out_ref, xx_scr = rest[n_alias:]
    nb = SAMPLE_NB
    r = nb * seq
    hist = 2 * SUBLANES

    u = jax.nn.gelu(u_ref[...])
    v = _layer_norm(jax.nn.gelu(v_ref[...]), lng_ref[...], lnb_ref[...])
    vn_ref[...] = v
    v3 = v.reshape(nb, seq, D_A)
    t_idx = lax.broadcasted_iota(I32, (seq, D_A), 0)
    mixed = jnp.broadcast_to(bmap_ref[...][None], (nb, seq, D_A))
    for s in range(seq):
        w_s = jnp.where(t_idx >= s, wmix_ref[s], 0.0)
        mixed = mixed + w_s[None] * v3[:, s:s + 1, :]
    a_ref[...] = (u * mixed.reshape(r, D_A)).astype(BF16)

    cos2 = cos_ref[...]
    sin2 = sin_ref[...]
    rowb = lax.broadcasted_iota(I32, (r, DV_RET), 0) // seq
    for h in range(H_RET):
        qs = slice(h * DK_RET, (h + 1) * DK_RET)
        vs = slice(h * DV_RET, (h + 1) * DV_RET)
        qr = _rotary(q_ref[:, qs], cos2, sin2)
        kr = _rotary(k_ref[:, qs], cos2, sin2) * (DK_RET ** -0.5)
        qb = qr.astype(BF16)
        vh = vr_ref[:, vs]
        scores = _dot_nt(qb, kr.astype(BF16)) * din_ref[h]
        inner = _dot(scores.astype(BF16), vh.astype(BF16))
        kd = (kr * dk_ref[h]).astype(BF16)
        cross = jnp.zeros((r, DV_RET), F32)
        for b in range(nb):
            s_old = s_in_ref[b, h]
            cross = jnp.where(rowb == b, _dot(qb, s_old.astype(BF16)), cross)
            vm = jnp.where(rowb == b, vh, 0.0).astype(BF16)
            s_out_ref[b, h] = dc_ref[h] * s_old + _dot_tn(kd, vm)
        oh = inner + cross * dq_ref[h]
        oh = oh * lax.rsqrt(jnp.mean(oh * oh, axis=-1, keepdims=True) + EPS)
        o_ref[:, vs] = (jax.nn.silu(gr_ref[:, vs]) * oh).astype(BF16)

    xc = xc_ref[...]
    xx_scr[:, hist - POOL_BUF:hist, :] = buf_in_ref[...]
    xx_scr[:, hist:hist + seq, :] = xc.reshape(nb, seq, D_C)
    pos1 = pos0 + 1 + lax.broadcasted_iota(I32, (seq, DG_C), 0)
    for gi, w in enumerate(POOL_WINDOWS):
        sl = slice(gi * DG_C, (gi + 1) * DG_C)
        acc = xx_scr[:, hist:hist + seq, sl]
        for j in range(1, w):
            acc = acc + xx_scr[:, hist - j:hist - j + seq, sl]
        cnt = jnp.minimum(w, pos1).astype(F32)
        pooled = (acc / cnt[None]).reshape(r, DG_C) - xc[:, sl]
        c_ref[:, sl] = (_dot(pooled.astype(BF16), wp_ref[gi]) * ps_ref[:, sl]).astype(BF16)
    buf_out_ref[...] = xx_scr[:, hist + seq - POOL_BUF:hist + seq, :]


def _mix_sample(z, a_buf, o_buf, c_buf, ret_buf, row0, batch, seq, pos0, tabs, lw,
                state_ret, state_pool, layer):
    nb = SAMPLE_NB
    r = nb * seq
    blk0 = row0 // r
    rows = batch * seq
    t = z.shape[0]

    def zspec(width, blk):
        return pl.BlockSpec((r, width), lambda i: (blk0 + i, blk))

    def whole(shape):
        nd = len(shape)
        return pl.BlockSpec((None,) + shape, lambda i: (layer,) + (0,) * nd)

    def const(shape):
        nd = len(shape)
        return pl.BlockSpec(shape, lambda i: (0,) * nd)

    in_specs = [
        zspec(D_A, BLK_U), zspec(D_A, BLK_V), zspec(QK_W, BLK_Q), zspec(QK_W, BLK_K),
        zspec(V_W, BLK_VR), zspec(V_W, BLK_GR), zspec(D_C, BLK_XC),
        const((r, DK_RET)), const((r, DK_RET)),
        whole((1, D_A)), whole((1, D_A)), whole((seq, seq, D_A)), whole((seq, D_A)),
        const((H_RET, r, r)), const((H_RET, r, 1)), const((H_RET, r, 1)), const((H_RET, 1, 1)),
        whole((len(POOL_WINDOWS), DG_C, DG_C)), whole((1, D_C)),
        pl.BlockSpec((None, nb, H_RET, DK_RET, DV_RET), lambda i: (layer, i, 0, 0, 0)),
        pl.BlockSpec((None, nb, POOL_BUF, D_C), lambda i: (layer, i, 0, 0)),
    ]
    args = [z, z, z, z, z, z, z, tabs["cos"], tabs["sin"],
            lw["ln_g"], lw["ln_b"], lw["wmix_s"], lw["bmap_s"],
            tabs["din"], tabs["dq"], tabs["dk"], tabs["dc"],
            lw["w_pool"], lw["pool_scale"], state_ret, state_pool]
    n_in = len(args)
    aliased = [a_buf, o_buf, c_buf] + ([ret_buf] if ret_buf is not None else [])
    aliases = {n_in + 0: 0, n_in + 1: 1, n_in + 2: 2}
    if ret_buf is not None:
        aliases[n_in + 3] = 4
    in_specs += [_any_spec()] * len(aliased)
    out_specs = [
        pl.BlockSpec((r, D_A), lambda i: (blk0 + i, 0)),
        pl.BlockSpec((r, V_W), lambda i: (blk0 + i, 0)),
        pl.BlockSpec((r, D_C), lambda i: (blk0 + i, 0)),
        pl.BlockSpec((r, D_A), lambda i: (i, 0)),
        pl.BlockSpec((None, nb, H_RET, DK_RET, DV_RET), lambda i: (layer, i, 0, 0, 0)),
        pl.BlockSpec((nb, POOL_BUF, D_C), lambda i: (i, 0, 0)),
    ]
    out_shape = [
        jax.ShapeDtypeStruct((t, D_A), BF16),
        jax.ShapeDtypeStruct((t, V_W), BF16),
        jax.ShapeDtypeStruct((t, D_C), BF16),
        jax.ShapeDtypeStruct((rows, D_A), F32),
        jax.ShapeDtypeStruct((DEPTH, batch, H_RET, DK_RET, DV_RET), F32),
        jax.ShapeDtypeStruct((batch, POOL_BUF, D_C), F32),
    ]
    return pl.pallas_call(
        functools.partial(_mix_sample_body, seq=seq, pos0=pos0, n_alias=len(aliased)),
        grid=(batch // nb,),
        in_specs=in_specs,
        out_specs=out_specs,
        out_shape=out_shape,
        scratch_shapes=[pltpu.VMEM((nb, 2 * SUBLANES + seq, D_C), F32)],
        input_output_aliases=aliases,
        compiler_params=_params(("parallel",), 40),
        name="mix_sample",
    )(*args, *aliased)


def _merge_body(a_ref, o_ref, c_ref, ga_ref, gb_ref, gc_ref, h_ref,
                wa_ref, wb_ref, wc_ref, wo_ref, out_ref):
    def gate(g_ref):
        return jax.nn.sigmoid(g_ref[...].astype(F32))

    m = gate(ga_ref) * _dot(a_ref[...], wa_ref[...])
    m = m + gate(gb_ref) * _dot(o_ref[...], wb_ref[...])
    m = m + gate(gc_ref) * _dot(c_ref[...], wc_ref[...])
    out_ref[...] = h_ref[...] + _dot(m.astype(BF16), wo_ref[...])


def _merge(a, o, c, sg, h, lw, layer):
    t = h.shape[0]
    tm = 512

    def rows(width, blk=0):
        return pl.BlockSpec((tm, width), lambda i: (i, blk))

    def whole(shape):
        nd = len(shape)
        return pl.BlockSpec((None,) + shape, lambda i: (layer,) + (0,) * nd)

    return pl.pallas_call(
        _merge_body,
        grid=(t // tm,),
        in_specs=[rows(D_A), rows(V_W), rows(D_C),
                  rows(D_MODEL, 0), rows(D_MODEL, 1), rows(D_MODEL, 2),
                  rows(D_MODEL),
                  whole((D_A, D_MODEL)), whole((V_W, D_MODEL)), whole((D_C, D_MODEL)),
                  whole((D_MODEL, D_MODEL))],
        out_specs=pl.BlockSpec((tm, D_MODEL), lambda i: (i, 0)),
        out_shape=jax.ShapeDtypeStruct((t, D_MODEL), F32),
        compiler_params=_params(("parallel",), 48),
        name="merge",
    )(a, o, c, sg, sg, sg, h, lw["w_br_a"], lw["w_br_b"], lw["w_br_c"], lw["w_out"])


def _ffn_body(x_ref, g_ref, gn_ref, wg_ref, wu_ref, wd_ref, out_ref, xnext_ref, xn_ref, acc_ref):
    j = pl.program_id(1)

    @pl.when(j == 0)
    def _():
        xn_ref[...] = _rms(x_ref[...], g_ref[...]).astype(BF16)
        acc_ref[...] = jnp.zeros_like(acc_ref)

    xn = xn_ref[...]
    hid = jax.nn.silu(_dot(xn, wg_ref[...])) * _dot(xn, wu_ref[...])
    acc_ref[...] += _dot(hid.astype(BF16), wd_ref[...])

    @pl.when(j == pl.num_programs(1) - 1)
    def _():
        out = x_ref[...] + acc_ref[...]
        out_ref[...] = out
        xnext_ref[...] = _rms(out, gn_ref[...]).astype(BF16)


def _ffn_dense(h, g_all, g_next_all, wg, wu, wd, layer, j):
    t = h.shape[0]
    tm, tf = 1024, 256
    return pl.pallas_call(
        _ffn_body,
        grid=(t // tm, D_FF // tf),
        in_specs=[
            pl.BlockSpec((tm, D_MODEL), lambda i, f: (i, 0)),
            pl.BlockSpec((None, 1, D_MODEL), lambda i, f: (layer, 0, 0)),
            pl.BlockSpec((None, 1, D_MODEL), lambda i, f: (layer + 1, 0, 0)),
            pl.BlockSpec((None, D_MODEL, tf), lambda i, f: (j, 0, f)),
            pl.BlockSpec((None, D_MODEL, tf), lambda i, f: (j, 0, f)),
            pl.BlockSpec((None, tf, D_MODEL), lambda i, f: (j, f, 0)),
        ],
        out_specs=[pl.BlockSpec((tm, D_MODEL), lambda i, f: (i, 0)),
                   pl.BlockSpec((tm, D_MODEL), lambda i, f: (i, 0))],
        out_shape=[jax.ShapeDtypeStruct((t, D_MODEL), F32),
                   jax.ShapeDtypeStruct((t, D_MODEL), BF16)],
        scratch_shapes=[pltpu.VMEM((tm, D_MODEL), BF16), pltpu.VMEM((tm, D_MODEL), F32)],
        compiler_params=_params(("parallel", "arbitrary"), 48),
        name="ffn_dense",
    )(h, g_all, g_next_all, wg, wu, wd)


def _router_body(x_ref, g_ref, wr_ref, idx_ref, wgt_ref, cnt_ref, carry_scr):
    @pl.when(pl.program_id(0) == 0)
    def _():
        carry_scr[...] = jnp.zeros_like(carry_scr)

    tm = x_ref.shape[0]
    xn = _rms(x_ref[...], g_ref[...])
    logits = jnp.dot(xn, wr_ref[...], preferred_element_type=F32, precision=lax.Precision.HIGHEST)
    lane = lax.broadcasted_iota(I32, logits.shape, 1)
    neg = jnp.float32(-jnp.inf)
    logits = jnp.where(lane < N_EXPERTS, logits, neg)
    v1 = jnp.max(logits, axis=-1, keepdims=True)
    i1 = jnp.min(jnp.where(logits == v1, lane, LANES), axis=-1, keepdims=True)
    rest = jnp.where(lane == i1, neg, logits)
    v2 = jnp.max(rest, axis=-1, keepdims=True)
    i2 = jnp.min(jnp.where(rest == v2, lane, LANES), axis=-1, keepdims=True)
    e2 = jnp.exp(v2 - v1)
    den = 1.0 + e2
    w1 = 1.0 / den
    w2 = e2 / den

    hit1 = lane == i1
    hit2 = lane == i2
    cnt = jnp.where(hit1, 1.0, jnp.where(hit2, 1.0, 0.0))
    r_i = lax.broadcasted_iota(I32, (tm, tm), 0)
    c_i = lax.broadcasted_iota(I32, (tm, tm), 1)
    strictly_lower = jnp.where(c_i < r_i, 1.0, 0.0).astype(BF16)
    before = _dot(strictly_lower, cnt.astype(BF16)) + carry_scr[...]
    rank1 = jnp.sum(jnp.where(hit1, before, 0.0), axis=-1, keepdims=True).astype(I32)
    rank2 = jnp.sum(jnp.where(hit2, before, 0.0), axis=-1, keepdims=True).astype(I32)
    carry_scr[...] += jnp.sum(cnt, axis=0, keepdims=True)

    idx_ref[...] = jnp.where(lane == 0, i1, jnp.where(lane == 1, i2,
                             jnp.where(lane == 2, rank1, jnp.where(lane == 3, rank2, 0))))
    wgt_ref[...] = jnp.where(lane == 0, w1, jnp.where(lane == 1, w2, 0.0))
    cnt_ref[...] = carry_scr[...].astype(I32)


def _router(h, g_all, wr_all, layer, j):
    t = h.shape[0]
    tm = ROW_TILE
    return pl.pallas_call(
        _router_body,
        grid=(t // tm,),
        in_specs=[
            pl.BlockSpec((tm, D_MODEL), lambda i: (i, 0)),
            pl.BlockSpec((None, 1, D_MODEL), lambda i: (layer, 0, 0)),
            pl.BlockSpec((None, D_MODEL, LANES), lambda i: (j, 0, 0)),
        ],
        out_specs=[pl.BlockSpec((tm, LANES), lambda i: (i, 0)),
                   pl.BlockSpec((tm, LANES), lambda i: (i, 0)),
                   pl.BlockSpec((1, LANES), lambda i: (0, 0))],
        out_shape=[jax.ShapeDtypeStruct((t, LANES), I32),
                   jax.ShapeDtypeStruct((t, LANES), F32),
                   jax.ShapeDtypeStruct((1, LANES), I32)],
        scratch_shapes=[pltpu.VMEM((1, LANES), F32)],
        compiler_params=_params(("arbitrary",), 32),
        name="router",
    )(h, g_all, wr_all)


def _routing_plan(idx, cnt, n_tiles):
    e1, e2, r1, r2 = idx[:, 0], idx[:, 1], idx[:, 2], idx[:, 3]
    counts = cnt[0, :N_EXPERTS]
    gsz = ((counts + MOE_TILE - 1) // MOE_TILE) * MOE_TILE
    gend = jnp.cumsum(gsz)
    gstart = gend - gsz
    pos1 = gstart[e1] + r1
    pos2 = gstart[e2] + r2
    n_used = gend[-1] // MOE_TILE
    tile_ids = jnp.arange(n_tiles, dtype=I32)
    texp = jnp.sum((tile_ids * MOE_TILE)[:, None] >= gend[None, :], axis=1).astype(I32)
    texp = jnp.minimum(texp, N_EXPERTS - 1)
    texp = jnp.where(tile_ids < n_used, texp, texp[jnp.maximum(n_used - 1, 0)])
    zrow = jnp.where(counts > 0, gend - MOE_TILE, -1)
    return pos1.astype(I32), pos2.astype(I32), texp, n_used.astype(I32).reshape(1), zrow.astype(I32)


TOKEN_ROWS = D_MODEL // LANES
assert TOKEN_ROWS == SUBLANES


def _token_tile(ref, token):
    return ref.at[pl.ds(pl.multiple_of(token * TOKEN_ROWS, TOKEN_ROWS), TOKEN_ROWS)]


def _store_token_major(dst_ref, x):
    n = x.shape[0]
    for c in range(TOKEN_ROWS):
        dst_ref[pl.ds(c, n, stride=TOKEN_ROWS), :] = x[:, c * LANES:(c + 1) * LANES]


def _load_token_major(src_ref, n):
    return jnp.concatenate([src_ref[pl.ds(c, n, stride=TOKEN_ROWS), :] for c in range(TOKEN_ROWS)],
                           axis=1)


def _dispatch_body(pos1_ref, pos2_ref, zrow_ref, h_ref, g_ref, xs_ref, hn_scr, sems):
    tm = h_ref.shape[0]
    i = pl.program_id(0)
    base = i * tm
    step_rows = tm * TOKEN_ROWS
    slot = i % 2
    buf = hn_scr.at[slot]

    @pl.when(i == 0)
    def _():
        zeros = hn_scr.at[1]
        zeros[...] = jnp.zeros(zeros.shape, F32)
        for e in range(N_EXPERTS):
            zrow = zrow_ref[e]

            @pl.when(zrow >= 0)
            def _():
                for part in range(MOE_TILE // tm):
                    row0 = pl.multiple_of((zrow + part * tm) * TOKEN_ROWS, TOKEN_ROWS)
                    cp = pltpu.make_async_copy(zeros, xs_ref.at[pl.ds(row0, step_rows)], sems.at[1, 0])
                    cp.start()
                    cp.wait()

    _store_token_major(buf, _rms(h_ref[...], g_ref[...]))

    def start(t, carry):
        src = _token_tile(buf, t)
        for k, pos_ref in enumerate((pos1_ref, pos2_ref)):
            pltpu.make_async_copy(src, _token_tile(xs_ref, pos_ref[base + t]),
                                  sems.at[slot, k]).start(priority=k)
        return carry

    lax.fori_loop(0, tm, start, 0, unroll=8)

    def drain(s):
        for k in range(TOP_K):
            pltpu.make_async_copy(hn_scr.at[s], xs_ref.at[pl.ds(0, step_rows)], sems.at[s, k]).wait()

    @pl.when(i > 0)
    def _():
        drain(1 - slot)

    @pl.when(i == pl.num_programs(0) - 1)
    def _():
        drain(slot)


def _dispatch(h, g_all, pos1, pos2, zrow, n_rows, layer):
    t = h.shape[0]
    tm = ROW_TILE
    return pl.pallas_call(
        _dispatch_body,
        grid_spec=pltpu.PrefetchScalarGridSpec(
            num_scalar_prefetch=3,
            grid=(t // tm,),
            in_specs=[pl.BlockSpec((tm, D_MODEL), lambda i, *_: (i, 0)),
                      pl.BlockSpec((None, 1, D_MODEL), lambda i, *_: (layer, 0, 0))],
            out_specs=_any_spec(),
            scratch_shapes=[pltpu.VMEM((2, tm * TOKEN_ROWS, LANES), F32),
                            pltpu.SemaphoreType.DMA((2, TOP_K))],
        ),
        out_shape=jax.ShapeDtypeStruct((n_rows * TOKEN_ROWS, LANES), F32),
        compiler_params=_params(("arbitrary",), 32),
        name="moe_dispatch",
    )(pos1, pos2, zrow, h, g_all)


def _moe_group_body(texp_ref, nused_ref, x_ref, wg_ref, wu_ref, wd_ref, y_ref, xb_ref, acc_ref):
    i = pl.program_id(0)
    f = pl.program_id(1)
    tm = xb_ref.shape[0]

    @pl.when(i < nused_ref[0])
    def _():
        @pl.when(f == 0)
        def _():
            xb_ref[...] = _load_token_major(x_ref, tm).astype(BF16)
            acc_ref[...] = jnp.zeros_like(acc_ref)

        xb = xb_ref[...]
        hid = jax.nn.silu(_dot(xb, wg_ref[...].astype(BF16))) * _dot(xb, wu_ref[...].astype(BF16))
        acc_ref[...] += _dot(hid.astype(BF16), wd_ref[...].astype(BF16))

        @pl.when(f == pl.num_programs(1) - 1)
        def _():
            _store_token_major(y_ref, acc_ref[...])


def _moe_group(xs, texp, n_used, wg, wu, wd, j):
    n_rows = xs.shape[0] // TOKEN_ROWS
    tm, tf = MOE_TILE, 512
    nf = D_FF_EXP // tf

    def row_blk(i, nu):
        return jnp.maximum(jnp.minimum(i, nu[0] - 1), 0)

    def f_blk(i, f, nu):
        return jnp.where(i < nu[0], f, nf - 1)

    return pl.pallas_call(
        _moe_group_body,
        grid_spec=pltpu.PrefetchScalarGridSpec(
            num_scalar_prefetch=2,
            grid=(n_rows // tm, nf),
            in_specs=[
                pl.BlockSpec((tm * TOKEN_ROWS, LANES), lambda i, f, te, nu: (row_blk(i, nu), 0)),
                pl.BlockSpec((None, None, D_MODEL, tf), lambda i, f, te, nu: (j, te[i], 0, f_blk(i, f, nu))),
                pl.BlockSpec((None, None, D_MODEL, tf), lambda i, f, te, nu: (j, te[i], 0, f_blk(i, f, nu))),
                pl.BlockSpec((None, None, tf, D_MODEL), lambda i, f, te, nu: (j, te[i], f_blk(i, f, nu), 0)),
            ],
            out_specs=pl.BlockSpec((tm * TOKEN_ROWS, LANES), lambda i, f, te, nu: (row_blk(i, nu), 0)),
            scratch_shapes=[pltpu.VMEM((tm, D_MODEL), BF16), pltpu.VMEM((tm, D_MODEL), F32)],
        ),
        out_shape=jax.ShapeDtypeStruct((n_rows * TOKEN_ROWS, LANES), F32),
        compiler_params=_params(("arbitrary", "arbitrary"), 52),
        name="moe_group",
    )(texp, n_used, xs, wg, wu, wd)


def _combine_body(pos1_ref, pos2_ref, h_ref, wgt_ref, gn_ref, ys_ref, *rest, final_prompt_blocks):
    *out_refs, y_scr, sems = rest
    tm = h_ref.shape[0]
    i = pl.program_id(0)
    slot = i % 2

    def fetch(step, s):
        base = step * tm

        def start(t, carry):
            for k, pos_ref in enumerate((pos1_ref, pos2_ref)):
                pltpu.make_async_copy(_token_tile(ys_ref, pos_ref[base + t]),
                                      _token_tile(y_scr.at[s, k], t), sems.at[s, k]).start(priority=k)
            return carry

        lax.fori_loop(0, tm, start, 0, unroll=8)

    @pl.when(i == 0)
    def _():
        fetch(0, 0)

    @pl.when(i + 1 < pl.num_programs(0))
    def _():
        fetch(i + 1, 1 - slot)

    for k in range(TOP_K):
        pltpu.make_async_copy(ys_ref.at[pl.ds(0, tm * TOKEN_ROWS)], y_scr.at[slot, k],
                              sems.at[slot, k]).wait()
    w = wgt_ref[...]
    out = h_ref[...] + (w[:, 0:1] * _load_token_major(y_scr.at[slot, 0], tm)
                        + w[:, 1:2] * _load_token_major(y_scr.at[slot, 1], tm))
    if final_prompt_blocks is None:
        out_refs[0][...] = out
        out_refs[1][...] = _rms(out, gn_ref[...]).astype(BF16)
    else:
        y = _rms(out, gn_ref[...])

        @pl.when(i < final_prompt_blocks)
        def _():
            out_refs[0][...] = y

        @pl.when(i >= final_prompt_blocks)
        def _():
            out_refs[1][...] = y


def _combine(h, wgt, g_next, ys, pos1, pos2, final_prompt_rows):
    t = h.shape[0]
    tm = COMBINE_TILE
    rows = pl.BlockSpec((tm, D_MODEL), lambda i, *_: (i, 0))
    if final_prompt_rows is None:
        pb = None
        out_specs = [rows, rows]
        out_shape = [jax.ShapeDtypeStruct((t, D_MODEL), F32), jax.ShapeDtypeStruct((t, D_MODEL), BF16)]
    else:
        pb = final_prompt_rows // tm
        out_specs = [pl.BlockSpec((tm, D_MODEL), lambda i, *_: (jnp.minimum(i, pb - 1), 0)),
                     pl.BlockSpec((tm, D_MODEL), lambda i, *_: (jnp.maximum(i - pb, 0), 0))]
        out_shape = [jax.ShapeDtypeStruct((final_prompt_rows, D_MODEL), F32),
                     jax.ShapeDtypeStruct((t - final_prompt_rows, D_MODEL), F32)]
    return pl.pallas_call(
        functools.partial(_combine_body, final_prompt_blocks=pb),
        grid_spec=pltpu.PrefetchScalarGridSpec(
            num_scalar_prefetch=2,
            grid=(t // tm,),
            in_specs=[rows,
                      pl.BlockSpec((tm, LANES), lambda i, *_: (i, 0)),
                      pl.BlockSpec((1, D_MODEL), lambda i, *_: (0, 0)),
                      _any_spec()],
            out_specs=out_specs,
            scratch_shapes=[pltpu.VMEM((2, TOP_K, tm * TOKEN_ROWS, LANES), F32),
                            pltpu.SemaphoreType.DMA((2, TOP_K))],
        ),
        out_shape=out_shape,
        compiler_params=_params(("arbitrary",), 32),
        name="moe_combine",
    )(pos1, pos2, h, wgt, g_next, ys)


def _ffn_moe(h, g_all, g_next, wr_all, wg, wu, wd, layer, j, final_prompt_rows):
    t = h.shape[0]
    n_rows = TOP_K * t + N_EXPERTS * MOE_TILE
    idx, wgt, cnt = _router(h, g_all, wr_all, layer, j)
    pos1, pos2, texp, n_used, zrow = _routing_plan(idx, cnt, n_rows // MOE_TILE)
    xs = _dispatch(h, g_all, pos1, pos2, zrow, n_rows, layer)
    ys = _moe_group(xs, texp, n_used, wg, wu, wd, j)
    return _combine(h, wgt, g_next, ys, pos1, pos2, final_prompt_rows)


def _rope_tables(pos):
    half = DK_RET // 2
    inv = ROPE_BASE ** (-jnp.arange(half, dtype=F32) / half)
    ang = pos.astype(F32)[:, None] * inv[None, :]
    cos, sin = jnp.cos(ang), jnp.sin(ang)
    return jnp.concatenate([cos, cos], axis=-1), jnp.concatenate([-sin, sin], axis=-1)


def _decay_tables(c):
    log_g = jnp.log1p(-jnp.exp2(-5.0 - jnp.arange(H_RET, dtype=F32)))
    idx = jnp.arange(c, dtype=F32)
    diff = idx[:, None] - idx[None, :]
    causal = diff >= 0
    d_in = jnp.where(causal[None], jnp.exp(log_g[:, None, None] * jnp.where(causal, diff, 0.0)[None]), 0.0)
    d_q = jnp.exp(log_g[:, None] * (idx[None, :] + 1.0))
    d_k = jnp.exp(log_g[:, None] * (c - 1.0 - idx[None, :]))
    d_c = jnp.exp(log_g * c)
    return d_in, d_q, d_k, d_c


def _prompt_tables(seq):
    c = min(CHUNK, seq)
    cos2, sin2 = _rope_tables(jnp.arange(seq, dtype=I32))
    d_in, d_q, d_k, d_c = _decay_tables(c)
    return dict(cos=cos2, sin=sin2, din=d_in, dq=d_q[:, :, None], dk=d_k[:, :, None],
                dc=d_c[:, None, None])


def _sample_tables(seq, pos0):
    nb = SAMPLE_NB
    cos2, sin2 = _rope_tables(pos0 + jnp.arange(seq, dtype=I32))
    d_in, d_q, d_k, d_c = _decay_tables(seq)
    eye = jnp.eye(nb, dtype=F32)
    d_blk = jnp.einsum("ab,hij->haibj", eye, d_in).reshape(H_RET, nb * seq, nb * seq)
    return dict(cos=jnp.tile(cos2, (nb, 1)), sin=jnp.tile(sin2, (nb, 1)), din=d_blk,
                dq=jnp.tile(d_q, (1, nb))[:, :, None], dk=jnp.tile(d_k, (1, nb))[:, :, None],
                dc=d_c[:, None, None])


def kernel(x_prompt, x_sample, state_ret, state_pool, norm1_g, w_in, ln_v_g, ln_v_b, w_spatial, b_spatial, w_pool, pool_scale, w_br_a, w_br_b, w_br_c, w_out, norm2_g, w_ffn_gate, w_ffn_up, w_ffn_down, w_router, w_exp_gate, w_exp_up, w_exp_down, final_norm_g):
    batch, seq, _ = x_prompt.shape
    dec_batch, dec_seq, _ = x_sample.shape
    n_prompt = batch * seq
    n_sample = dec_batch * dec_seq
    assert seq % (PROMPT_CHUNKS_PER_STEP * CHUNK) == 0 and dec_seq <= CHUNK and PAST_LEN % CHUNK == 0
    assert n_prompt % (SAMPLE_NB * dec_seq) == 0 and dec_batch % SAMPLE_NB == 0

    lw = dict(
        ln_g=ln_v_g[:, None, :], ln_b=ln_v_b[:, None, :],
        w_spatial=w_spatial,
        bmap_p=jnp.repeat(jnp.swapaxes(b_spatial, 1, 2), DG_A, axis=2),
        wmix_s=jnp.repeat(jnp.transpose(w_spatial[:, :, :dec_seq, :dec_seq], (0, 3, 2, 1)), DG_A, axis=3),
        w_pool=w_pool.astype(BF16), pool_scale=pool_scale[:, None, :],
        w_br_a=w_br_a.astype(BF16), w_br_b=w_br_b.astype(BF16), w_br_c=w_br_c.astype(BF16),
        w_out=w_out.astype(BF16),
    )
    lw["bmap_s"] = lw["bmap_p"][:, :dec_seq, :]
    g1 = norm1_g[:, None, :]
    g2 = norm2_g[:, None, :]
    wfg, wfu, wfd = w_ffn_gate.astype(BF16), w_ffn_up.astype(BF16), w_ffn_down.astype(BF16)
    wr = jnp.pad(w_router, ((0, 0), (0, 0), (0, LANES - N_EXPERTS)))
    tabs_p = _prompt_tables(seq)
    tabs_s = _sample_tables(dec_seq, PAST_LEN)

    ret_p, pool_p, pool_s, v_s = [], [], [], []
    ret_s = None
    assert DEPTH % 2 == 0
    h, xn = _norm_cast(x_prompt.reshape(n_prompt, D_MODEL), x_sample.reshape(n_sample, D_MODEL), g1, 0)
    for layer in range(DEPTH):
        j = layer // 2
        last = layer == DEPTH - 1
        z = _inproj(xn, w_in, layer, gates=False)
        sg = _inproj(xn, w_in, layer, gates=True)
        a, o, c, s_p, b_p = _mix_prompt(z, batch, seq, tabs_p, lw, layer)
        a, o, c, vn, ret_s, b_s = _mix_sample(z, a, o, c, ret_s, n_prompt, dec_batch, dec_seq, PAST_LEN,
                                              tabs_s, lw, state_ret, state_pool, layer)
        h = _merge(a, o, c, sg, h, lw, layer)
        if layer % 2 == 0:
            h, xn = _ffn_dense(h, g2, g1, wfg, wfu, wfd, layer, j)
        else:
            g_next = final_norm_g[None, :] if last else g1[layer + 1]
            res = _ffn_moe(h, g2, g_next, wr, w_exp_gate, w_exp_up, w_exp_down, layer, j,
                           n_prompt if last else None)
            if last:
                y_p, y_s = res
            else:
                h, xn = res
        ret_p.append(s_p)
        pool_p.append(b_p)
        pool_s.append(b_s)
        v_s.append(vn.reshape(dec_batch, dec_seq, D_A))

    y_prompt = y_p.reshape(batch, seq, D_MODEL)
    y_sample = y_s.reshape(dec_batch, dec_seq, D_MODEL)
    return (y_prompt, y_sample, jnp.stack(ret_p), ret_s, jnp.stack(pool_p),
            jnp.stack(pool_s), jnp.stack(v_s))
```
